```python
import jax, jax.numpy as jnp
from jax import lax
import numpy as np

D_MODEL = 2048
BATCH = 4
SEQ = 8192
DEPTH = 1

PLE_DIM = 256
RMS_EPS = 1e-6

GLA_HEADS = 4
GLA_KEY = D_MODEL // 2
GLA_VAL = D_MODEL
GLA_DK = GLA_KEY // GLA_HEADS
GLA_DV = GLA_VAL // GLA_HEADS
GLA_GATE_RANK = 16
GLA_TAU = 16.0
GLA_CHUNK = 64
GLA_NORM_EPS = 1e-5

RWKV_HEAD = 64
RWKV_HEADS = D_MODEL // RWKV_HEAD
RWKV_WIDTH = RWKV_HEADS * RWKV_HEAD
RWKV_DECAY_RANK = max(32, int(round(1.8 * D_MODEL ** 0.5 / 32)) * 32)
RWKV_A_RANK = max(32, int(round(1.8 * D_MODEL ** 0.5 / 32)) * 32)
RWKV_G_RANK = max(32, int(round(0.6 * D_MODEL ** 0.8 / 32)) * 32)
RWKV_LN_EPS = 64e-5
RWKV_SPLITS = (RWKV_WIDTH, RWKV_WIDTH, RWKV_WIDTH, RWKV_DECAY_RANK, RWKV_A_RANK, RWKV_G_RANK)
RWKV_SHIFT = sum(RWKV_SPLITS)

IN_SPLITS = (GLA_KEY, GLA_KEY, GLA_VAL, GLA_GATE_RANK, GLA_VAL, RWKV_SHIFT, D_MODEL, D_MODEL)
IN_WIDTH = sum(IN_SPLITS)

N_EXPERTS = 64
TOP_K = 8
N_GROUPS = 8
TOPK_GROUPS = 4
EXPERT_FF = D_MODEL // 4
SHARED_FF = EXPERT_FF
ROUTED_SCALE = 2.5
MOE_BLOCK = 256

kernel_name = "hybrid_gla_rwkv7_moe_ple_block"


def rms_norm(x, g, eps=RMS_EPS):
    xf = x.astype(jnp.float32)
    y = xf * lax.rsqrt(jnp.mean(xf * xf, axis=-1, keepdims=True) + eps) * g.astype(jnp.float32)
    return y.astype(x.dtype)


def token_shift(z):
    return jnp.pad(z, ((0, 0), (1, 0), (0, 0)))[:, :-1]


def split_cols(z, sizes):
    idx = np.cumsum(sizes)[:-1].tolist()
    return jnp.split(z, idx, axis=-1)


def gla_mix(q, k, v, gate_lr, out_gate, w_a2, b_a, o_norm):
    f32 = jnp.float32
    B_, S_, _ = q.shape
    nc = S_ // GLA_CHUNK
    log_a = jax.nn.log_sigmoid((gate_lr @ w_a2 + b_a).astype(f32)) / GLA_TAU

    def chunks(t, d):
        return t.astype(f32).reshape(B_, nc, GLA_CHUNK, GLA_HEADS, d).transpose(0, 3, 1, 2, 4)

    qc = chunks(q, GLA_DK) * (GLA_DK ** -0.5)
    kc = chunks(k, GLA_DK)
    vc = chunks(v, GLA_DV)
    bcum = jnp.cumsum(chunks(log_a, GLA_DK), axis=3)
    btot = bcum[..., -1:, :]
    q_t = qc * jnp.exp(bcum)
    k_t = kc * jnp.exp(-bcum)
    k_d = kc * jnp.exp(btot - bcum)
    causal = jnp.tril(jnp.ones((GLA_CHUNK, GLA_CHUNK), dtype=bool))
    att = jnp.where(causal, jnp.einsum('bhntd,bhnsd->bhnts', q_t, k_t), 0.0)
    o_intra = jnp.einsum('bhnts,bhnse->bhnte', att, vc)

    def step(state, inp):
        qn, kn, vn, dn = inp
        o = jnp.einsum('bhtd,bhde->bhte', qn, state)
        state = state * dn[..., None] + jnp.einsum('bhsd,bhse->bhde', kn, vn)
        return state, o

    s0 = jnp.zeros((B_, GLA_HEADS, GLA_DK, GLA_DV), f32)
    mv = lambda t: jnp.moveaxis(t, 2, 0)
    _, o_inter = lax.scan(step, s0, (mv(q_t), mv(k_d), mv(vc), mv(jnp.exp(btot[..., 0, :]))))
    o = o_intra + jnp.moveaxis(o_inter, 0, 2)
    o = o.transpose(0, 2, 3, 1, 4).reshape(B_, S_, GLA_HEADS, GLA_DV)
    o = o * lax.rsqrt(jnp.mean(o * o, axis=-1, keepdims=True) + GLA_NORM_EPS) * o_norm.astype(f32)
    return o.reshape(B_, S_, GLA_VAL) * jax.nn.silu(out_gate.astype(f32))


def rwkv7_mix(r, k, v, w_lr, a_lr, g_lr, w0, w_w2, a0, w_a2, w_g2, k_k, k_a, r_k, ln_w, ln_b):
    f32 = jnp.float32
    B_, S_, C = r.shape
    r, k, v = r.astype(f32), k.astype(f32), v.astype(f32)
    w_log = -jax.nn.softplus(-(w0 + jnp.tanh(w_lr) @ w_w2).astype(f32)) - 0.5
    decay = jnp.exp(-jnp.exp(w_log))
    icl = jax.nn.sigmoid((a0 + a_lr @ w_a2).astype(f32))
    g = (jax.nn.sigmoid(g_lr) @ w_g2).astype(f32)
    heads = lambda t: t.reshape(B_, S_, RWKV_HEADS, RWKV_HEAD)
    kk = heads(k * k_k)
    kk = kk / jnp.maximum(jnp.sqrt(jnp.sum(kk * kk, axis=-1, keepdims=True)), 1e-12)
    k = k * (1.0 + (icl - 1.0) * k_a)
    rh, wh, kh, vh, ah = heads(r), heads(decay), heads(k), heads(v), heads(icl)
    vec_a = -kk
    vec_b = kk * ah

    def step(state, inp):
        r_t, w_t, k_t, v_t, a_t, b_t = inp
        sa = jnp.einsum('bhij,bhj->bhi', state, a_t)
        state = (state * w_t[:, :, None, :] + sa[..., None] * b_t[:, :, None, :]
                 + v_t[..., None] * k_t[:, :, None, :])
        return state, jnp.einsum('bhij,bhj->bhi', state, r_t)

    s0 = jnp.zeros((B_, RWKV_HEADS, RWKV_HEAD, RWKV_HEAD), f32)
    sf = lambda t: jnp.moveaxis(t, 1, 0)
    _, y = lax.scan(step, s0, (sf(rh), sf(wh), sf(kh), sf(vh), sf(vec_a), sf(vec_b)))
    y = jnp.moveaxis(y, 0, 1)
    mean = jnp.mean(y, axis=-1, keepdims=True)
    var = jnp.mean(jnp.square(y - mean), axis=-1, keepdims=True)
    y = ((y - mean) * lax.rsqrt(var + RWKV_LN_EPS)).reshape(B_, S_, C) * ln_w + ln_b
    bonus = jnp.sum(rh * kh * r_k, axis=-1, keepdims=True) * vh
    return (y + bonus.reshape(B_, S_, C)) * g


def moe_ffn(h, w_router, router_bias, w_gate, w_up, w_down, ws_gate, ws_up, ws_down):
    B_, S_, D = h.shape
    xs = h.reshape(B_ * S_, D)
    N = xs.shape[0]
    scores = jax.nn.sigmoid(jnp.einsum('nd,de->ne', xs, w_router).astype(jnp.float32))
    biased = scores + router_bias.astype(jnp.float32)
    grp_score = lax.top_k(biased.reshape(N, N_GROUPS, N_EXPERTS // N_GROUPS), 2)[0].sum(-1)
    _, grp_idx = lax.top_k(grp_score, TOPK_GROUPS)
    grp_mask = jnp.any(grp_idx[:, :, None] == jnp.arange(N_GROUPS)[None, None, :], axis=1)
    expert_mask = jnp.repeat(grp_mask, N_EXPERTS // N_GROUPS, axis=1)
    _, top_idx = lax.top_k(jnp.where(expert_mask, biased, -jnp.inf), TOP_K)
    top_s = jnp.take_along_axis(scores, top_idx, axis=1)
    gates = top_s / jnp.sum(top_s, axis=-1, keepdims=True) * ROUTED_SCALE

    nk = N * TOP_K
    flat_e = top_idx.reshape(nk).astype(jnp.int32)
    flat_tok = jnp.arange(nk, dtype=jnp.int32) // TOP_K
    flat_g = gates.reshape(nk)
    order = jnp.argsort(flat_e)
    sorted_e = flat_e[order]
    counts = jnp.bincount(flat_e, length=N_EXPERTS)
    padded = (counts + MOE_BLOCK - 1) // MOE_BLOCK * MOE_BLOCK
    start = jnp.cumsum(counts) - counts
    pend = jnp.cumsum(padded)
    pstart = pend - padded
    dest = pstart[sorted_e] + jnp.arange(nk, dtype=jnp.int32) - start[sorted_e]
    n_blocks = -(-nk // MOE_BLOCK) + N_EXPERTS
    n_slots = n_blocks * MOE_BLOCK
    slot_tok = jnp.zeros((n_slots,), jnp.int32).at[dest].set(flat_tok[order])
    slot_gate = jnp.zeros((n_slots,), jnp.float32).at[dest].set(flat_g[order])
    block_e = jnp.minimum(jnp.searchsorted(pend, jnp.arange(n_blocks, dtype=jnp.int32) * MOE_BLOCK, side='right'),
                          N_EXPERTS - 1)

    def expert_block(acc, blk):
        tok, g, e = blk
        xb = xs[tok]
        hb = jax.nn.silu(xb @ w_gate[e]) * (xb @ w_up[e])
        yb = (hb @ w_down[e]).astype(jnp.float32) * g[:, None]
        return acc.at[tok].add(yb), None

    routed, _ = lax.scan(expert_block, jnp.zeros((N, D), jnp.float32),
                         (slot_tok.reshape(n_blocks, MOE_BLOCK), slot_gate.reshape(n_blocks, MOE_BLOCK), block_e))
    shared = (jax.nn.silu(xs @ ws_gate) * (xs @ ws_up)) @ ws_down
    return (routed + shared.astype(jnp.float32)).reshape(B_, S_, D)


def setup_inputs(seed: int = 0) -> dict:
    key = jax.random.key(seed)
    ks = jax.random.split(key, 40)
    f32 = jnp.float32
    L, D = DEPTH, D_MODEL
    nrm = lambda i, shape, s: jax.random.normal(ks[i], shape, f32) * s
    gain = lambda i, shape: 1.0 + 0.02 * jax.random.normal(ks[i], shape, f32)
    w0_base = jnp.linspace(-6.5, -1.5, RWKV_WIDTH, dtype=f32)[None, :]
    return {
        'x': nrm(0, (BATCH, SEQ, D), 1.0),
        'p': nrm(1, (L, BATCH, SEQ, PLE_DIM), 1.0),
        'norm_mix': gain(2, (L, D)),
        'w_in': nrm(3, (L, D, IN_WIDTH), D ** -0.5),
        'mu_shift': jax.random.uniform(ks[4], (L, RWKV_SHIFT), f32),
        'gla_wa2': nrm(5, (L, GLA_GATE_RANK, GLA_KEY), GLA_GATE_RANK ** -0.5),
        'gla_ba': 2.0 + nrm(6, (L, GLA_KEY), 0.5),
        'gla_onorm': gain(7, (L, GLA_DV)),
        'rwkv_w0': w0_base + nrm(8, (L, RWKV_WIDTH), 0.1),
        'rwkv_ww2': nrm(9, (L, RWKV_DECAY_RANK, RWKV_WIDTH), 0.1),
        'rwkv_a0': nrm(10, (L, RWKV_WIDTH), 0.1),
        'rwkv_wa2': nrm(11, (L, RWKV_A_RANK, RWKV_WIDTH), 0.5 * RWKV_A_RANK ** -0.5),
        'rwkv_wg2': nrm(12, (L, RWKV_G_RANK, RWKV_WIDTH), RWKV_G_RANK ** -0.5),
        'rwkv_kk': 0.85 + nrm(13, (L, RWKV_WIDTH), 0.02),
        'rwkv_ka': gain(14, (L, RWKV_WIDTH)),
        'rwkv_rk': nrm(15, (L, RWKV_HEADS, RWKV_HEAD), 0.1),
        'rwkv_lnw': gain(16, (L, RWKV_WIDTH)),
        'rwkv_lnb': nrm(17, (L, RWKV_WIDTH), 0.02),
        'w_o': nrm(18, (L, D, D), D ** -0.5),
        'norm_ffn': gain(19, (L, D)),
        'w_router': nrm(20, (L, D, N_EXPERTS), D ** -0.5),
        'router_bias': nrm(21, (L, N_EXPERTS), 0.01),
        'w_exp_gate': nrm(22, (L, N_EXPERTS, D, EXPERT_FF), D ** -0.5),
        'w_exp_up': nrm(23, (L, N_EXPERTS, D, EXPERT_FF), D ** -0.5),
        'w_exp_down': nrm(24, (L, N_EXPERTS, EXPERT_FF, D), EXPERT_FF ** -0.5),
        'w_sh_gate': nrm(25, (L, D, SHARED_FF), D ** -0.5),
        'w_sh_up': nrm(26, (L, D, SHARED_FF), D ** -0.5),
        'w_sh_down': nrm(27, (L, SHARED_FF, D), SHARED_FF ** -0.5),
        'norm_ple': gain(28, (L, D)),
        'w_ple_gate': nrm(29, (L, D, D), D ** -0.5),
        'w_ple_proj': nrm(30, (L, PLE_DIM, D), PLE_DIM ** -0.5),
        'norm_final': gain(31, (D,)),
    }


def reference(x, p, norm_mix, w_in, mu_shift, gla_wa2, gla_ba, gla_onorm, rwkv_w0, rwkv_ww2, rwkv_a0,
              rwkv_wa2, rwkv_wg2, rwkv_kk, rwkv_ka, rwkv_rk, rwkv_lnw, rwkv_lnb, w_o, norm_ffn, w_router,
              router_bias, w_exp_gate, w_exp_up, w_exp_down, w_sh_gate, w_sh_up, w_sh_down, norm_ple,
              w_ple_gate, w_ple_proj, norm_final):
    B_, S_, D = x.shape
    for i in range(DEPTH):
        h = rms_norm(x, norm_mix[i])
        proj = jnp.einsum('bsd,dc->bsc', h, w_in[i])
        q_a, k_a, v_a, glr_a, og_a, rw, mg_a, mg_b = split_cols(proj, IN_SPLITS)
        rw = rw + (token_shift(rw) - rw) * mu_shift[i]
        r_b, k_b, v_b, wlr_b, alr_b, glr_b = split_cols(rw, RWKV_SPLITS)
        o_a = gla_mix(q_a, k_a, v_a, glr_a, og_a, gla_wa2[i], gla_ba[i], gla_onorm[i])
        o_b = rwkv7_mix(r_b, k_b, v_b, wlr_b, alr_b, glr_b, rwkv_w0[i], rwkv_ww2[i], rwkv_a0[i], rwkv_wa2[i],
                        rwkv_wg2[i], rwkv_kk[i], rwkv_ka[i], rwkv_rk[i], rwkv_lnw[i], rwkv_lnb[i])
        mixed = (jax.nn.sigmoid(mg_a.astype(jnp.float32)) * o_a
                 + jax.nn.sigmoid(mg_b.astype(jnp.float32)) * o_b)
        x = x + jnp.einsum('bsc,cd->bsd', mixed.astype(x.dtype), w_o[i]).astype(x.dtype)
        h2 = rms_norm(x, norm_ffn[i])
        x = x + moe_ffn(h2, w_router[i], router_bias[i], w_exp_gate[i], w_exp_up[i], w_exp_down[i],
                        w_sh_gate[i], w_sh_up[i], w_sh_down[i]).astype(x.dtype)
        gate = jax.nn.sigmoid(jnp.einsum('bsd,de->bse', rms_norm(x, norm_ple[i]), w_ple_gate[i]).astype(jnp.float32))
        emb = jnp.einsum('bsk,kd->bsd', p[i], w_ple_proj[i]).astype(jnp.float32)
        x = x + (gate * emb).astype(x.dtype)
    return rms_norm(x, norm_final)
```

```python
import functools

import jax
import jax.numpy as jnp
from jax import lax
from jax.experimental import pallas as pl
from jax.experimental.pallas import tpu as pltpu

F32 = jnp.float32
BF16 = jnp.bfloat16
HIGHEST = lax.Precision.HIGHEST

RMS_EPS = 1e-6
GLA_HEADS = 4
GLA_GATE_RANK = 16
GLA_TAU = 16.0
GLA_NORM_EPS = 1e-5
RWKV_HEAD = 64
RWKV_LN_EPS = 64e-5
N_EXPERTS = 64
TOP_K = 8
N_GROUPS = 8
TOPK_GROUPS = 4
ROUTED_SCALE = 2.5
MOE_BLOCK = 256
CHUNK = 64
LANES = 128
VMEM_LIMIT = 56 * 1024 * 1024

NT = (((1,), (1,)), ((), ()))
TN = (((0,), (0,)), ((), ()))


def _params(sem, vmem=VMEM_LIMIT):
    return pltpu.CompilerParams(dimension_semantics=sem, vmem_limit_bytes=vmem)


def _bdot(a, b):
    return jnp.dot(a.astype(BF16), b.astype(BF16), preferred_element_type=F32)


def _softplus(y):
    return jnp.maximum(y, 0.0) + jnp.log1p(jnp.exp(-jnp.abs(y)))


def _in_proj_kernel(x_ref, g_ref, w_ref, o_ref, h_ref):
    @pl.when(pl.program_id(1) == 0)
    def _():
        x = x_ref[...]
        ms = jnp.mean(x * x, axis=-1, keepdims=True)
        h_ref[...] = (x * lax.rsqrt(ms + RMS_EPS) * g_ref[...]).astype(BF16)

    o_ref[...] = jnp.dot(h_ref[...], w_ref[...], preferred_element_type=F32).astype(o_ref.dtype)


def _in_proj(x2d, gain, w_cat, tm, tn):
    n, d = x2d.shape
    m = w_cat.shape[1]
    return pl.pallas_call(
        _in_proj_kernel,
        grid=(n // tm, m // tn),
        in_specs=[
            pl.BlockSpec((tm, d), lambda i, j: (i, 0)),
            pl.BlockSpec((1, d), lambda i, j: (0, 0)),
            pl.BlockSpec((d, tn), lambda i, j: (0, j)),
        ],
        out_specs=pl.BlockSpec((tm, tn), lambda i, j: (i, j)),
        out_shape=jax.ShapeDtypeStruct((n, m), BF16),
        scratch_shapes=[pltpu.VMEM((tm, d), BF16)],
        compiler_params=_params(("parallel", "arbitrary")),
        name="in_proj",
    )(x2d, gain, w_cat)


def _gla_kernel(q_ref, k_ref, v_ref, og_ref, glr_ref, wa2_ref, ba_ref, on_ref, o_ref, st_ref, *, dk):
    @pl.when(pl.program_id(2) == 0)
    def _():
        st_ref[...] = jnp.zeros_like(st_ref)

    tb = q_ref.shape[0]
    row = lax.broadcasted_iota(jnp.int32, (CHUNK, CHUNK), 0)
    col = lax.broadcasted_iota(jnp.int32, (CHUNK, CHUNK), 1)
    causal = row >= col
    tri = causal.astype(F32)
    scale = dk ** -0.5
    for c in range(tb // CHUNK):
        sl = pl.ds(c * CHUNK, CHUNK)
        q = q_ref[sl, :].astype(F32) * scale
        k = k_ref[sl, :].astype(F32)
        v = v_ref[sl, :]
        x = jnp.dot(glr_ref[sl, :], wa2_ref[...], preferred_element_type=F32) + ba_ref[...]
        log_a = (jnp.minimum(x, 0.0) - jnp.log1p(jnp.exp(-jnp.abs(x)))) / GLA_TAU
        bcum = jnp.dot(tri, log_a, preferred_element_type=F32, precision=HIGHEST)
        btot = bcum[CHUNK - 1:CHUNK, :]
        q_t = (q * jnp.exp(bcum)).astype(BF16)
        k_t = (k * jnp.exp(-bcum)).astype(BF16)
        k_d = (k * jnp.exp(btot - bcum)).astype(BF16)
        att = lax.dot_general(q_t, k_t, NT, preferred_element_type=F32)
        att = jnp.where(causal, att, 0.0).astype(BF16)
        st = st_ref[...]
        o = jnp.dot(att, v, preferred_element_type=F32)
        o = o + lax.dot_general(q_t, st.astype(BF16), NT, preferred_element_type=F32)
        st_ref[...] = st * jnp.exp(btot) + lax.dot_general(v, k_d, TN, preferred_element_type=F32)
        ms = jnp.mean(o * o, axis=-1, keepdims=True)
        o = o * lax.rsqrt(ms + GLA_NORM_EPS) * on_ref[...]
        og = og_ref[sl, :].astype(F32)
        o_ref[sl, :] = (o * (og * jax.nn.sigmoid(og))).astype(o_ref.dtype)


def _gla(proj, wa2p, ba, onorm, *, batch, seq, d_model, tb, col_small_glr):
    heads = GLA_HEADS
    dk = d_model // 2 // heads
    dv = d_model // heads
    n = batch * seq
    nt = seq // tb
    rowmap = lambda b, h, t: b * nt + t
    kcol = (d_model // 2) // dk
    vcol = d_model // dv
    ogcol = 2 * d_model // dv
    return pl.pallas_call(
        functools.partial(_gla_kernel, dk=dk),
        grid=(batch, heads, nt),
        in_specs=[
            pl.BlockSpec((tb, dk), lambda b, h, t: (rowmap(b, h, t), h)),
            pl.BlockSpec((tb, dk), lambda b, h, t: (rowmap(b, h, t), kcol + h)),
            pl.BlockSpec((tb, dv), lambda b, h, t: (rowmap(b, h, t), vcol + h)),
            pl.BlockSpec((tb, dv), lambda b, h, t: (rowmap(b, h, t), ogcol + h)),
            pl.BlockSpec((tb, LANES), lambda b, h, t: (rowmap(b, h, t), col_small_glr // LANES)),
            pl.BlockSpec((LANES, dk), lambda b, h, t: (0, h)),
            pl.BlockSpec((1, dk), lambda b, h, t: (0, h)),
            pl.BlockSpec((1, dv), lambda b, h, t: (0, 0)),
        ],
        out_specs=pl.BlockSpec((tb, dv), lambda b, h, t: (rowmap(b, h, t), h)),
        out_shape=jax.ShapeDtypeStruct((n, d_model), BF16),
        scratch_shapes=[pltpu.VMEM((dv, dk), F32)],
        compiler_params=_params(("parallel", "parallel", "arbitrary")),
        name="gla",
    )(proj, proj, proj, proj, proj, wa2p, ba, onorm)


def _shift_mix(z, prev_ref, mu, first_row):
    zs = pltpu.roll(z, 1, 0)
    zs = jnp.where(first_row, prev_ref[0:1, :], zs)
    prev_ref[0:1, :] = z[z.shape[0] - 1:z.shape[0], :]
    return z + (zs - z) * mu


def _rwkv_kernel(r_ref, k_ref, v_ref, sm_ref, mur_ref, muk_ref, muv_ref, musm_ref,
                 ww2_ref, wa2_ref, wg2_ref, w0_ref, a0_ref, kk_ref, ka_ref, rk_ref, lnw_ref, lnb_ref,
                 o_ref, ht_ref, pr_ref, pk_ref, pv_ref, psm_ref):
    @pl.when(pl.program_id(2) == 0)
    def _():
        ht_ref[...] = jnp.zeros_like(ht_ref)
        pr_ref[...] = jnp.zeros_like(pr_ref)
        pk_ref[...] = jnp.zeros_like(pk_ref)
        pv_ref[...] = jnp.zeros_like(pv_ref)
        psm_ref[...] = jnp.zeros_like(psm_ref)

    tb, gw = r_ref.shape
    hd = RWKV_HEAD
    first_row = lax.broadcasted_iota(jnp.int32, (tb, 1), 0) == 0
    r = _shift_mix(r_ref[...].astype(F32), pr_ref, mur_ref[...], first_row)
    k = _shift_mix(k_ref[...].astype(F32), pk_ref, muk_ref[...], first_row)
    v = _shift_mix(v_ref[...].astype(F32), pv_ref, muv_ref[...], first_row)
    sm = _shift_mix(sm_ref[...].astype(F32), psm_ref, musm_ref[...], first_row)
    w_lr = sm[:, 0:LANES]
    a_lr = sm[:, LANES:2 * LANES]
    g_lr = sm[:, 2 * LANES:]

    w_log = -_softplus(-(w0_ref[...] + _bdot(jnp.tanh(w_lr), ww2_ref[...]))) - 0.5
    lw = -jnp.exp(w_log)
    icl = jax.nn.sigmoid(a0_ref[...] + _bdot(a_lr, wa2_ref[...]))
    g = _bdot(jax.nn.sigmoid(g_lr), wg2_ref[...])

    lane_r = lax.broadcasted_iota(jnp.int32, (gw, gw), 0) // hd
    lane_c = lax.broadcasted_iota(jnp.int32, (gw, gw), 1) // hd
    same_head = lane_r == lane_c
    head_sum = same_head.astype(F32)

    def hsum(x):
        return jnp.dot(x, head_sum, preferred_element_type=F32, precision=HIGHEST)

    kk = k * kk_ref[...]
    kk = kk / jnp.maximum(jnp.sqrt(hsum(kk * kk)), 1e-12)
    k2 = k * (1.0 + (icl - 1.0) * ka_ref[...])
    a = -kk
    b = kk * icl
    bonus = hsum(r * k2 * rk_ref[...]) * v

    head0 = lax.broadcasted_iota(jnp.int32, (1, gw), 1) < hd
    rr = lax.broadcasted_iota(jnp.int32, (gw, gw), 0) % CHUNK
    cc = lax.broadcasted_iota(jnp.int32, (gw, gw), 1) % CHUNK
    strict = rr > cc
    incl = rr >= cc
    eye = (lax.broadcasted_iota(jnp.int32, (gw, gw), 0)
           == lax.broadcasted_iota(jnp.int32, (gw, gw), 1)).astype(F32)
    tri = (lax.broadcasted_iota(jnp.int32, (CHUNK, CHUNK), 0)
           >= lax.broadcasted_iota(jnp.int32, (CHUNK, CHUNK), 1)).astype(F32)

    def split_heads(z):
        return jnp.concatenate([jnp.where(head0, z, 0.0), jnp.where(head0, 0.0, z)], axis=0)

    ys = []
    for c in range(tb // CHUNK):
        lo, hi = c * CHUNK, (c + 1) * CHUNK
        lw_c = lw[lo:hi]
        cum = jnp.dot(tri, lw_c, preferred_element_type=F32, precision=HIGHEST)
        tot = cum[CHUNK - 1:CHUNK, :]
        e_neg = jnp.exp(-cum)
        e_rem = jnp.exp(tot - cum)
        a_t = a[lo:hi] * jnp.exp(cum - lw_c)
        r_t = r[lo:hi] * jnp.exp(cum)
        b_t = b[lo:hi] * e_neg
        k_t = k2[lo:hi] * e_neg
        b_h = b[lo:hi] * e_rem
        k_h = k2[lo:hi] * e_rem
        v_c = v[lo:hi]

        lhs = jnp.concatenate([split_heads(a_t), split_heads(r_t)], axis=0).astype(BF16)
        rhs = jnp.concatenate([split_heads(b_t), split_heads(k_t)], axis=0).astype(BF16)
        pm = lax.dot_general(lhs, rhs, NT, preferred_element_type=F32)
        a_ab = jnp.where(strict, pm[0:gw, 0:gw], 0.0)
        a_ak = jnp.where(strict, pm[0:gw, gw:], 0.0)
        a_rb = jnp.where(incl, pm[gw:, 0:gw], 0.0)
        a_rk = jnp.where(incl, pm[gw:, gw:], 0.0)

        x = a_ab
        t_inv = eye + a_ab
        for _ in range(5):
            x = _bdot(x, x)
            t_inv = t_inv + _bdot(t_inv, x)

        ht = ht_ref[...]
        gq = lax.dot_general(lhs, ht.astype(BF16), NT, preferred_element_type=F32)
        v_st = split_heads(v_c)
        u = _bdot(t_inv, gq[0:gw] + _bdot(a_ak, v_st))
        y_st = gq[gw:] + _bdot(jnp.concatenate([a_rb, a_rk], axis=1),
                               jnp.concatenate([u, v_st], axis=0))
        ys.append(y_st[0:CHUNK] + y_st[CHUNK:])
        u_c = u[0:CHUNK] + u[CHUNK:]
        upd = lax.dot_general(jnp.concatenate([u_c, v_c], axis=0).astype(BF16),
                              jnp.concatenate([b_h, k_h], axis=0).astype(BF16),
                              TN, preferred_element_type=F32)
        ht_ref[...] = ht * jnp.exp(tot) + jnp.where(same_head, upd, 0.0)

    y = jnp.concatenate(ys, axis=0)
    mean = hsum(y) * (1.0 / hd)
    dlt = y - mean
    var = hsum(dlt * dlt) * (1.0 / hd)
    y = dlt * lax.rsqrt(var + RWKV_LN_EPS) * lnw_ref[...] + lnb_ref[...]
    o_ref[...] = ((y + bonus) * g).astype(o_ref.dtype)


def _rwkv(proj, mu_r, mu_k, mu_v, mu_sm, ww2p, wa2p, wg2, w0, a0, kkp, kap, rkp, lnw, lnb,
          *, batch, seq, d_model, tb, col_r, col_small):
    n = batch * seq
    nt = seq // tb
    gw = LANES
    ng = d_model // gw
    smw = mu_sm.shape[1]
    rowmap = lambda b, g, t: b * nt + t
    seg = lambda off: (lambda b, g, t: (rowmap(b, g, t), off // gw + g))
    vec = pl.BlockSpec((1, gw), lambda b, g, t: (0, g))
    return pl.pallas_call(
        _rwkv_kernel,
        grid=(batch, ng, nt),
        in_specs=[
            pl.BlockSpec((tb, gw), seg(col_r)),
            pl.BlockSpec((tb, gw), seg(col_r + d_model)),
            pl.BlockSpec((tb, gw), seg(col_r + 2 * d_model)),
            pl.BlockSpec((tb, smw), lambda b, g, t: (rowmap(b, g, t), col_small // smw)),
            vec, vec, vec,
            pl.BlockSpec((1, smw), lambda b, g, t: (0, 0)),
            pl.BlockSpec((LANES, gw), lambda b, g, t: (0, g)),
            pl.BlockSpec((LANES, gw), lambda b, g, t: (0, g)),
            pl.BlockSpec((wg2.shape[0], gw), lambda b, g, t: (0, g)),
            vec, vec, vec, vec, vec, vec, vec,
        ],
        out_specs=pl.BlockSpec((tb, gw), lambda b, g, t: (rowmap(b, g, t), g)),
        out_shape=jax.ShapeDtypeStruct((n, d_model), BF16),
        scratch_shapes=[
            pltpu.VMEM((gw, gw), F32),
            pltpu.VMEM((8, gw), F32), pltpu.VMEM((8, gw), F32), pltpu.VMEM((8, gw), F32),
            pltpu.VMEM((8, smw), F32),
        ],
        compiler_params=_params(("parallel", "parallel", "arbitrary")),
        name="rwkv",
    )(proj, proj, proj, proj, mu_r, mu_k, mu_v, mu_sm, ww2p, wa2p, wg2, w0, a0, kkp, kap, rkp, lnw, lnb)


def _merge_kernel(x_ref, oa_ref, ob_ref, ga_ref, gb_ref, wo_ref, g_ref, wr_ref, x1_ref, h2_ref, lg_ref):
    mixed = (jax.nn.sigmoid(ga_ref[...].astype(F32)) * oa_ref[...].astype(F32)
             + jax.nn.sigmoid(gb_ref[...].astype(F32)) * ob_ref[...].astype(F32))
    x1 = x_ref[...] + jnp.dot(mixed.astype(BF16), wo_ref[...], preferred_element_type=F32)
    x1_ref[...] = x1
    ms = jnp.mean(x1 * x1, axis=-1, keepdims=True)
    h2 = x1 * lax.rsqrt(ms + RMS_EPS) * g_ref[...]
    h2_ref[...] = h2
    lg_ref[...] = lax.dot_general(wr_ref[...], h2, NT, preferred_element_type=F32, precision=HIGHEST)


def _merge(x2d, o_a, o_b, proj, w_o, gain, w_router_t, *, tm, col_mga):
    n, d = x2d.shape
    ne = w_router_t.shape[0]
    row = lambda i: (i, 0)
    return pl.pallas_call(
        _merge_kernel,
        grid=(n // tm,),
        in_specs=[
            pl.BlockSpec((tm, d), row),
            pl.BlockSpec((tm, d), row),
            pl.BlockSpec((tm, d), row),
            pl.BlockSpec((tm, d), lambda i: (i, col_mga // d)),
            pl.BlockSpec((tm, d), lambda i: (i, col_mga // d + 1)),
            pl.BlockSpec((d, d), lambda i: (0, 0)),
            pl.BlockSpec((1, d), lambda i: (0, 0)),
            pl.BlockSpec((ne, d), lambda i: (0, 0)),
        ],
        out_specs=[
            pl.BlockSpec((tm, d), row),
            pl.BlockSpec((tm, d), row),
            pl.BlockSpec((ne, tm), lambda i: (0, i)),
        ],
        out_shape=[
            jax.ShapeDtypeStruct((n, d), F32),
            jax.ShapeDtypeStruct((n, d), F32),
            jax.ShapeDtypeStruct((ne, n), F32),
        ],
        compiler_params=_params(("parallel",)),
        name="merge",
    )(x2d, o_a, o_b, proj, proj, w_o, gain, w_router_t)


def _route_kernel(lg_ref, bias_ref, eidx_ref, gate_ref):
    ne, tr = lg_ref.shape
    gsz = ne // N_GROUPS
    neg = -jnp.inf
    scores = jax.nn.sigmoid(lg_ref[...])
    biased = scores + bias_ref[...]

    def first_max(vals, iota, size):
        m = jnp.max(vals, axis=0, keepdims=True)
        return m, jnp.min(jnp.where(vals == m, iota, float(size)), axis=0, keepdims=True)

    in_iota = lax.broadcasted_iota(jnp.int32, (gsz, tr), 0).astype(F32)
    slabs, gs_rows = [], []
    for grp in range(N_GROUPS):
        slab = biased[grp * gsz:(grp + 1) * gsz, :]
        m1, i1 = first_max(slab, in_iota, gsz)
        m2 = jnp.max(jnp.where(in_iota == i1, neg, slab), axis=0, keepdims=True)
        slabs.append(slab)
        gs_rows.append(m1 + m2)
    gs = jnp.concatenate(gs_rows, axis=0)
    g_iota = lax.broadcasted_iota(jnp.int32, (N_GROUPS, tr), 0).astype(F32)
    gsel = jnp.zeros((N_GROUPS, tr), F32)
    for _ in range(TOPK_GROUPS):
        _, gi = first_max(gs, g_iota, N_GROUPS)
        hit = g_iota == gi
        gsel = jnp.where(hit, 1.0, gsel)
        gs = jnp.where(hit, neg, gs)
    cand = jnp.concatenate(
        [jnp.where(gsel[grp:grp + 1, :] > 0.0, slabs[grp], neg) for grp in range(N_GROUPS)], axis=0)
    e_iota = lax.broadcasted_iota(jnp.int32, (ne, tr), 0).astype(F32)
    idxs, tops = [], []
    for _ in range(TOP_K):
        _, ei = first_max(cand, e_iota, ne)
        hit = e_iota == ei
        idxs.append(ei)
        tops.append(jnp.sum(jnp.where(hit, scores, 0.0), axis=0, keepdims=True))
        cand = jnp.where(hit, neg, cand)
    top_s = jnp.concatenate(tops, axis=0)
    eidx_ref[...] = jnp.concatenate(idxs, axis=0).astype(jnp.int32)
    gate_ref[...] = top_s / jnp.sum(top_s, axis=0, keepdims=True) * ROUTED_SCALE


def _route(logits_t, bias_col, *, tr):
    ne, n = logits_t.shape
    return pl.pallas_call(
        _route_kernel,
        grid=(n // tr,),
        in_specs=[pl.BlockSpec((ne, tr), lambda i: (0, i)), pl.BlockSpec((ne, 1), lambda i: (0, 0))],
        out_specs=[pl.BlockSpec((TOP_K, tr), lambda i: (0, i)), pl.BlockSpec((TOP_K, tr), lambda i: (0, i))],
        out_shape=[jax.ShapeDtypeStruct((TOP_K, n), jnp.int32), jax.ShapeDtypeStruct((TOP_K, n), F32)],
        compiler_params=_params(("parallel",)),
        name="route",
    )(logits_t, bias_col)


def _dispatch_tables(eidx, n_tokens):
    n = n_tokens
    nk = n * TOP_K
    n_blocks = -(-nk // MOE_BLOCK) + N_EXPERTS
    tok = jnp.broadcast_to(jnp.arange(n, dtype=jnp.int32)[None, :], (TOP_K, n))
    krow = jnp.arange(TOP_K, dtype=jnp.int32)[:, None]
    real_key = (eidx * (2 * n) + tok).reshape(nk)
    real_dest = (krow * n + tok).reshape(nk)
    counts = jnp.sum(eidx.reshape(1, nk) == jnp.arange(N_EXPERTS, dtype=jnp.int32)[:, None], axis=1)
    padded = (counts + MOE_BLOCK - 1) // MOE_BLOCK * MOE_BLOCK
    n_pad = n_blocks * MOE_BLOCK - nk
    pad_e = jnp.arange(n_pad, dtype=jnp.int32) // MOE_BLOCK
    pad_i = jnp.arange(n_pad, dtype=jnp.int32) % MOE_BLOCK
    pad_used = pad_i < (padded - counts)[jnp.minimum(pad_e, N_EXPERTS - 1)]
    pad_used = jnp.logical_and(pad_used, pad_e < N_EXPERTS)
    pad_key = jnp.where(pad_used, pad_e * (2 * n) + n + pad_i, jnp.iinfo(jnp.int32).max)
    pad_dest = nk + pad_i
    keys, dests = lax.sort_key_val(jnp.concatenate([real_key, pad_key]), jnp.concatenate([real_dest, pad_dest]))
    pend_blocks = jnp.cumsum(padded) // MOE_BLOCK
    block_e = jnp.minimum(
        jnp.searchsorted(pend_blocks, jnp.arange(n_blocks, dtype=jnp.int32), side='right'), N_EXPERTS - 1)
    n_valid = pend_blocks[-1:]
    return (keys.reshape(n_blocks, 1, MOE_BLOCK), dests.reshape(n_blocks, 1, MOE_BLOCK),
            block_e.astype(jnp.int32), n_valid.astype(jnp.int32))


def _experts_kernel(be_ref, nv_ref, key_ref, keyn_ref, dest_ref, h_hbm, wg_ref, wu_ref, wd_ref, out_hbm,
                    xbuf, ybuf, gsem, ssem, *, n_tokens):
    j = pl.program_id(0)
    nv = nv_ref[0]
    slot = j % 2

    def row_copy_in(keys, r, s):
        tok = jnp.minimum(lax.rem(keys[0, r], 2 * n_tokens), n_tokens - 1)
        return pltpu.make_async_copy(h_hbm.at[pl.ds(tok, 1), :], xbuf.at[s, pl.ds(r, 1), :], gsem.at[s])

    def row_copy_out(r, s):
        return pltpu.make_async_copy(ybuf.at[s, pl.ds(r, 1), :], out_hbm.at[pl.ds(dest_ref[0, r], 1), :],
                                     ssem.at[s])

    def gather_start(keys, s):
        lax.fori_loop(0, MOE_BLOCK, lambda r, c: (row_copy_in(keys, r, s).start(), c)[1], 0)

    def gather_wait(s):
        lax.fori_loop(0, MOE_BLOCK, lambda r, c: (row_copy_in(key_ref, r, s).wait(), c)[1], 0)

    def scatter_wait(s):
        lax.fori_loop(0, MOE_BLOCK, lambda r, c: (row_copy_out(r, s).wait(), c)[1], 0)

    @pl.when(j == 0)
    def _():
        ybuf[1] = jnp.zeros(ybuf.shape[1:], ybuf.dtype)
        spare = pltpu.make_async_copy(ybuf.at[1], out_hbm.at[pl.ds(TOP_K * n_tokens, MOE_BLOCK), :], ssem.at[1])
        spare.start()
        spare.wait()

    @pl.when(j < nv)
    def _():
        @pl.when(j == 0)
        def _():
            gather_start(key_ref, 0)

        @pl.when(j + 1 < nv)
        def _():
            gather_start(keyn_ref, 1 - slot)

        gather_wait(slot)

        @pl.when(j >= 2)
        def _():
            scatter_wait(slot)

        xb = xbuf[slot].astype(BF16)
        hg = jnp.dot(xb, wg_ref[...], preferred_element_type=F32)
        hu = jnp.dot(xb, wu_ref[...], preferred_element_type=F32)
        hb = (hg * jax.nn.sigmoid(hg) * hu).astype(BF16)
        ybuf[slot] = jnp.dot(hb, wd_ref[...], preferred_element_type=F32)
        lax.fori_loop(0, MOE_BLOCK, lambda r, c: (row_copy_out(r, slot).start(), c)[1], 0)

        @pl.when(j == nv - 1)
        def _():
            scatter_wait(slot)

            @pl.when(j >= 1)
            def _():
                scatter_wait(1 - slot)


def _experts(h2, keys, dests, block_e, n_valid, wg, wu, wd):
    n, d = h2.shape
    n_blocks = keys.shape[0]
    ff = wg.shape[2]
    smem_blk = lambda f: pl.BlockSpec((None, 1, MOE_BLOCK), f, memory_space=pltpu.SMEM)
    grid_spec = pltpu.PrefetchScalarGridSpec(
        num_scalar_prefetch=2,
        grid=(n_blocks,),
        in_specs=[
            smem_blk(lambda j, be, nv: (j, 0, 0)),
            smem_blk(lambda j, be, nv: (jnp.minimum(j + 1, n_blocks - 1), 0, 0)),
            smem_blk(lambda j, be, nv: (j, 0, 0)),
            pl.BlockSpec(memory_space=pl.ANY),
            pl.BlockSpec((None, d, ff), lambda j, be, nv: (be[j], 0, 0)),
            pl.BlockSpec((None, d, ff), lambda j, be, nv: (be[j], 0, 0)),
            pl.BlockSpec((None, ff, d), lambda j, be, nv: (be[j], 0, 0)),
        ],
        out_specs=pl.BlockSpec(memory_space=pl.ANY),
        scratch_shapes=[
            pltpu.VMEM((2, MOE_BLOCK, d), F32),
            pltpu.VMEM((2, MOE_BLOCK, d), F32),
            pltpu.SemaphoreType.DMA((2,)),
            pltpu.SemaphoreType.DMA((2,)),
        ],
    )
    return pl.pallas_call(
        functools.partial(_experts_kernel, n_tokens=n),
        grid_spec=grid_spec,
        out_shape=jax.ShapeDtypeStruct((TOP_K * n + MOE_BLOCK, d), F32),
        compiler_params=_params(("arbitrary",)),
        name="experts",
    )(block_e, n_valid, keys, keys, dests, h2, wg, wu, wd)


def _combine_kernel(x1_ref, h2_ref, gate_ref, *rest):
    y_refs = rest[:TOP_K]
    wg_ref, wu_ref, wd_ref, o_ref = rest[TOP_K:]
    gates = jnp.transpose(gate_ref[...])
    acc = x1_ref[...]
    for kk in range(TOP_K):
        acc = acc + gates[:, kk:kk + 1] * y_refs[kk][...]
    hb = h2_ref[...].astype(BF16)
    hg = jnp.dot(hb, wg_ref[...], preferred_element_type=F32)
    hu = jnp.dot(hb, wu_ref[...], preferred_element_type=F32)
    sh = jnp.dot((hg * jax.nn.sigmoid(hg) * hu).astype(BF16), wd_ref[...], preferred_element_type=F32)
    o_ref[...] = acc + sh


def _combine(x1, h2, gate8, y8, wsg, wsu, wsd, *, tm):
    n, d = x1.shape
    ff = wsg.shape[1]
    nb = n // tm
    row = lambda i: (i, 0)
    y_specs = [pl.BlockSpec((tm, d), functools.partial(lambda i, kk: (kk * nb + i, 0), kk=kk))
               for kk in range(TOP_K)]
    return pl.pallas_call(
        _combine_kernel,
        grid=(nb,),
        in_specs=[pl.BlockSpec((tm, d), row), pl.BlockSpec((tm, d), row),
                  pl.BlockSpec((TOP_K, tm), lambda i: (0, i))] + y_specs + [
                  pl.BlockSpec((d, ff), lambda i: (0, 0)), pl.BlockSpec((d, ff), lambda i: (0, 0)),
                  pl.BlockSpec((ff, d), lambda i: (0, 0))],
        out_specs=pl.BlockSpec((tm, d), row),
        out_shape=jax.ShapeDtypeStruct((n, d), F32),
        compiler_params=_params(("parallel",)),
        name="combine",
    )(x1, h2, gate8, *([y8] * TOP_K), wsg, wsu, wsd)


def _ple_kernel(x_ref, p_ref, gp_ref, wg_ref, wp_ref, *rest):
    o_ref = rest[-1]
    x = x_ref[...]
    ms = jnp.mean(x * x, axis=-1, keepdims=True)
    hn = (x * lax.rsqrt(ms + RMS_EPS) * gp_ref[...]).astype(BF16)
    gate = jax.nn.sigmoid(jnp.dot(hn, wg_ref[...], preferred_element_type=F32))
    emb = jnp.dot(p_ref[...].astype(BF16), wp_ref[...], preferred_element_type=F32)
    x = x + gate * emb
    if len(rest) == 2:
        ms = jnp.mean(x * x, axis=-1, keepdims=True)
        x = x * lax.rsqrt(ms + RMS_EPS) * rest[0][...]
    o_ref[...] = x


def _ple(x2, p2d, g_ple, w_pg, w_pp, g_final, *, tm):
    n, d = x2.shape
    pd = p2d.shape[1]
    row = lambda i: (i, 0)
    fix = lambda i: (0, 0)
    final = [] if g_final is None else [g_final]
    return pl.pallas_call(
        _ple_kernel,
        grid=(n // tm,),
        in_specs=[pl.BlockSpec((tm, d), row), pl.BlockSpec((tm, pd), row), pl.BlockSpec((1, d), fix),
                  pl.BlockSpec((d, d), fix), pl.BlockSpec((pd, d), fix)]
                 + [pl.BlockSpec((1, d), fix)] * len(final),
        out_specs=pl.BlockSpec((tm, d), row),
        out_shape=jax.ShapeDtypeStruct((n, d), F32),
        compiler_params=_params(("parallel",)),
        name="ple",
    )(x2, p2d, g_ple, w_pg, w_pp, *final)


def _pad_cols(w, width):
    return jnp.pad(w, ((0, 0), (0, width - w.shape[1])))


def _pad_rows(w, height):
    return jnp.pad(w, ((0, height - w.shape[0]), (0, 0)))


def _layer(x2d, p2d, lw, *, batch, seq, tiles):
    n, d = x2d.shape
    dk_all = d // 2
    w_in = lw['w_in']
    decay_rank = lw['rwkv_ww2'].shape[0]
    a_rank = lw['rwkv_wa2'].shape[0]
    g_rank = lw['rwkv_wg2'].shape[0]
    rw_width = 3 * d + decay_rank + a_rank + g_rank
    sizes = (dk_all, dk_all, d, GLA_GATE_RANK, d, rw_width, d, d)
    offs = [0]
    for s in sizes:
        offs.append(offs[-1] + s)
    seg = lambda i: w_in[:, offs[i]:offs[i + 1]]
    w_q, w_k, w_v, w_glr, w_og, w_rw, w_mga, w_mgb = (seg(i) for i in range(8))
    o_wlr, o_alr, o_glr = 3 * d, 3 * d + decay_rank, 3 * d + decay_rank + a_rank
    small = jnp.concatenate([
        _pad_cols(w_rw[:, o_wlr:o_alr], LANES), _pad_cols(w_rw[:, o_alr:o_glr], LANES), w_rw[:, o_glr:],
        _pad_cols(w_glr, LANES)], axis=1)
    small_w = 2 * LANES + g_rank
    tn = tiles['in_tn']
    small = _pad_cols(small, -(-small.shape[1] // tn) * tn)
    col_mga = 3 * d
    col_r = 5 * d
    col_small = 8 * d
    w_cat = jnp.concatenate([w_q, w_k, w_v, w_og, w_mga, w_mgb, w_rw[:, :3 * d], small], axis=1).astype(BF16)

    proj = _in_proj(x2d, lw['norm_mix'][None, :], w_cat, tiles['in_tm'], tn)

    o_a = _gla(proj, _pad_rows(lw['gla_wa2'], LANES).astype(BF16), lw['gla_ba'][None, :],
               lw['gla_onorm'][None, :], batch=batch, seq=seq, d_model=d, tb=tiles['gla_tb'],
               col_small_glr=col_small + small_w)

    mu = lw['mu_shift']
    mu_sm = jnp.concatenate([_pad_cols(mu[None, o_wlr:o_alr], LANES), _pad_cols(mu[None, o_alr:o_glr], LANES),
                             mu[None, o_glr:]], axis=1)
    rowv = lambda v: v.reshape(1, -1)
    o_b = _rwkv(proj, mu[None, 0:d], mu[None, d:2 * d], mu[None, 2 * d:3 * d], mu_sm,
                _pad_rows(lw['rwkv_ww2'], LANES).astype(BF16), _pad_rows(lw['rwkv_wa2'], LANES).astype(BF16),
                lw['rwkv_wg2'].astype(BF16), rowv(lw['rwkv_w0']), rowv(lw['rwkv_a0']), rowv(lw['rwkv_kk']),
                rowv(lw['rwkv_ka']), rowv(lw['rwkv_rk']), rowv(lw['rwkv_lnw']), rowv(lw['rwkv_lnb']),
                batch=batch, seq=seq, d_model=d, tb=tiles['rwkv_tb'], col_r=col_r, col_small=col_small)

    x1, h2, logits_t = _merge(x2d, o_a, o_b, proj, lw['w_o'].astype(BF16), lw['norm_ffn'][None, :],
                              jnp.transpose(lw['w_router']), tm=tiles['merge_tm'], col_mga=col_mga)

    eidx, gate8 = _route(logits_t, lw['router_bias'][:, None], tr=tiles['route_tr'])
    keys, dests, block_e, n_valid = _dispatch_tables(eidx, n)
    y8 = _experts(h2, keys, dests, block_e, n_valid, lw['w_exp_gate'].astype(BF16),
                  lw['w_exp_up'].astype(BF16), lw['w_exp_down'].astype(BF16))
    x2 = _combine(x1, h2, gate8, y8, lw['w_sh_gate'].astype(BF16), lw['w_sh_up'].astype(BF16),
                  lw['w_sh_down'].astype(BF16), tm=tiles['combine_tm'])
    return x2


_TILES = dict(in_tm=1024, in_tn=1024, gla_tb=256, rwkv_tb=256, merge_tm=256, route_tr=512,
              combine_tm=128, ple_tm=512)


def _forward(x, p, weights, norm_final, tiles):
    batch, seq, d = x.shape
    n = batch * seq
    depth = p.shape[0]
    x2d = x.reshape(n, d)
    for i in range(depth):
        lw = {k: v[i] for k, v in weights.items()}
        x2 = _layer(x2d, p[i].reshape(n, -1), lw, batch=batch, seq=seq, tiles=tiles)
        last = norm_final[None, :] if i == depth - 1 else None
        x2d = _ple(x2, p[i].reshape(n, -1), lw['norm_ple'][None, :], lw['w_ple_gate'].astype(BF16),
                   lw['w_ple_proj'].astype(BF16), last, tm=tiles['ple_tm'])
    return x2d.reshape(batch, seq, d)


def kernel(x, p, norm_mix, w_in, mu_shift, gla_wa2, gla_ba, gla_onorm, rwkv_w0, rwkv_ww2, rwkv_a0, rwkv_wa2, rwkv_wg2, rwkv_kk, rwkv_ka, rwkv_rk, rwkv_lnw, rwkv_lnb, w_o, norm_ffn, w_router, router_bias, w_exp_gate, w_exp_up, w_exp_down, w_sh_gate, w_sh_up, w_sh_down, norm_ple, w_ple_gate, w_ple_proj, norm_final):
    weights = dict(norm_mix=norm_mix, w_in=w_in, mu_shift=mu_shift, gla_wa2=gla_wa2, gla_ba=gla_ba,
                   gla_onorm=gla_onorm, rwkv_w0=rwkv_w0, rwkv_ww2=rwkv_ww2, rwkv_a0=rwkv_a0, rwkv_wa2=rwkv_wa2,
                   rwkv_wg2=rwkv_wg2, rwkv_kk=rwkv_kk, rwkv_ka=rwkv_ka, rwkv_rk=rwkv_rk, rwkv_lnw=rwkv_lnw,
                   rwkv_lnb=rwkv_lnb, w_o=w_o, norm_ffn=norm_ffn, w_router=w_router, router_bias=router_bias,
                   w_exp_gate=w_exp_gate, w_exp_up=w_exp_up, w_exp_down=w_exp_down, w_sh_gate=w_sh_gate,
                   w_sh_up=w_sh_up, w_sh_down=w_sh_down, norm_ple=norm_ple, w_ple_gate=w_ple_gate,
                   w_ple_proj=w_ple_proj)
    return _forward(x, p, weights, norm_final, _TILES)
```

```python
import functools

import jax
import jax.numpy as jnp
from jax import lax
from jax.experimental import pallas as pl
from jax.experimental.pallas import tpu as pltpu

F32 = jnp.float32
BF16 = jnp.bfloat16
HIGHEST = lax.Precision.HIGHEST

RMS_EPS = 1e-6
GLA_HEADS = 4
GLA_GATE_RANK = 16
GLA_TAU = 16.0
GLA_NORM_EPS = 1e-5
RWKV_HEAD = 64
RWKV_LN_EPS = 64e-5
N_EXPERTS = 64
TOP_K = 8
N_GROUPS = 8
TOPK_GROUPS = 4
ROUTED_SCALE = 2.5
MOE_BLOCK = 256
CHUNK = 64
LANES = 128
VMEM_LIMIT = 56 * 1024 * 1024

NT = (((1,), (1,)), ((), ()))
TN = (((0,), (0,)), ((), ()))


def _params(sem, vmem=VMEM_LIMIT):
    return pltpu.CompilerParams(dimension_semantics=sem, vmem_limit_bytes=vmem)


def _bdot(a, b):
    return jnp.dot(a.astype(BF16), b.astype(BF16), preferred_element_type=F32)


def _bf16_pieces(x, n):
    pieces = []
    for _ in range(n):
        p = x.astype(BF16)
        pieces.append(p)
        x = x - p.astype(F32)
    return pieces


def _cumsum_rows(tri3, x):
    return jnp.dot(tri3, jnp.concatenate(_bf16_pieces(x, 3), axis=0), preferred_element_type=F32)


def _tri3(c):
    tri = (lax.broadcasted_iota(jnp.int32, (c, c), 0) >= lax.broadcasted_iota(jnp.int32, (c, c), 1)).astype(BF16)
    return jnp.concatenate([tri, tri, tri], axis=1)


def _softplus(y):
    return jnp.maximum(y, 0.0) + jnp.log1p(jnp.exp(-jnp.abs(y)))


def _in_proj_kernel(x_ref, g_ref, w_ref, o_ref, h_ref):
    @pl.when(pl.program_id(1) == 0)
    def _():
        x = x_ref[...]
        ms = jnp.mean(x * x, axis=-1, keepdims=True)
        h_ref[...] = (x * lax.rsqrt(ms + RMS_EPS) * g_ref[...]).astype(BF16)

    o_ref[...] = jnp.dot(h_ref[...], w_ref[...], preferred_element_type=F32).astype(o_ref.dtype)


def _in_proj(x2d, gain, w_cat, tm, tn):
    n, d = x2d.shape
    m = w_cat.shape[1]
    return pl.pallas_call(
        _in_proj_kernel,
        grid=(n // tm, m // tn),
        in_specs=[
            pl.BlockSpec((tm, d), lambda i, j: (i, 0)),
            pl.BlockSpec((1, d), lambda i, j: (0, 0)),
            pl.BlockSpec((d, tn), lambda i, j: (0, j)),
        ],
        out_specs=pl.BlockSpec((tm, tn), lambda i, j: (i, j)),
        out_shape=jax.ShapeDtypeStruct((n, m), BF16),
        scratch_shapes=[pltpu.VMEM((tm, d), BF16)],
        compiler_params=_params(("parallel", "arbitrary")),
        name="in_proj",
    )(x2d, gain, w_cat)


def _gla_kernel(q_ref, k_ref, v_ref, og_ref, glr_ref, wa2_ref, ba_ref, on_ref, o_ref, st_ref, *, dk):
    @pl.when(pl.program_id(2) == 0)
    def _():
        st_ref[...] = jnp.zeros_like(st_ref)

    tb = q_ref.shape[0]
    row = lax.broadcasted_iota(jnp.int32, (CHUNK, CHUNK), 0)
    col = lax.broadcasted_iota(jnp.int32, (CHUNK, CHUNK), 1)
    causal = row >= col
    tri3 = _tri3(CHUNK)
    scale = dk ** -0.5
    rows = [pl.ds(c * CHUNK, CHUNK) for c in range(tb // CHUNK)]
    la = jnp.dot(glr_ref[...], wa2_ref[...], preferred_element_type=F32) + ba_ref[...]
    la = (jnp.minimum(la, 0.0) - jnp.log1p(jnp.exp(-jnp.abs(la)))) / GLA_TAU
    bcums = [_cumsum_rows(tri3, la[c * CHUNK:(c + 1) * CHUNK]) for c in range(len(rows))]
    q_t, k_t, k_d, decay = [], [], [], []
    for sl, bcum in zip(rows, bcums):
        k = k_ref[sl, :].astype(F32)
        btot = bcum[CHUNK - 1:CHUNK, :]
        q_t.append((q_ref[sl, :].astype(F32) * scale * jnp.exp(bcum)).astype(BF16))
        k_t.append((k * jnp.exp(-bcum)).astype(BF16))
        k_d.append((k * jnp.exp(btot - bcum)).astype(BF16))
        decay.append(jnp.exp(btot))
    atts = [lax.dot_general(qt, kt, NT, preferred_element_type=F32) for qt, kt in zip(q_t, k_t)]
    atts = [jnp.where(causal, att, 0.0).astype(BF16) for att in atts]
    upds = [lax.dot_general(v_ref[sl, :], kd, TN, preferred_element_type=F32) for sl, kd in zip(rows, k_d)]
    states = [st_ref[...]]
    for dec, upd in zip(decay, upds):
        states.append(states[-1] * dec + upd)
    st_ref[...] = states[-1]
    outs = [jnp.dot(att, v_ref[sl, :], preferred_element_type=F32)
            + lax.dot_general(qt, st.astype(BF16), NT, preferred_element_type=F32)
            for att, sl, qt, st in zip(atts, rows, q_t, states)]
    for sl, o in zip(rows, outs):
        ms = jnp.mean(o * o, axis=-1, keepdims=True)
        o = o * lax.rsqrt(ms + GLA_NORM_EPS) * on_ref[...]
        og = og_ref[sl, :].astype(F32)
        o_ref[sl, :] = (o * (og * jax.nn.sigmoid(og))).astype(o_ref.dtype)


def _gla(proj, wa2p, ba, onorm, *, batch, seq, d_model, tb, col_small_glr):
    heads = GLA_HEADS
    dk = d_model // 2 // heads
    dv = d_model // heads
    n = batch * seq
    nt = seq // tb
    rowmap = lambda b, h, t: b * nt + t
    kcol = (d_model // 2) // dk
    vcol = d_model // dv
    ogcol = 2 * d_model // dv
    return pl.pallas_call(
        functools.partial(_gla_kernel, dk=dk),
        grid=(batch, heads, nt),
        in_specs=[
            pl.BlockSpec((tb, dk), lambda b, h, t: (rowmap(b, h, t), h)),
            pl.BlockSpec((tb, dk), lambda b, h, t: (rowmap(b, h, t), kcol + h)),
            pl.BlockSpec((tb, dv), lambda b, h, t: (rowmap(b, h, t), vcol + h)),
            pl.BlockSpec((tb, dv), lambda b, h, t: (rowmap(b, h, t), ogcol + h)),
            pl.BlockSpec((tb, LANES), lambda b, h, t: (rowmap(b, h, t), col_small_glr // LANES)),
            pl.BlockSpec((LANES, dk), lambda b, h, t: (0, h)),
            pl.BlockSpec((1, dk), lambda b, h, t: (0, h)),
            pl.BlockSpec((1, dv), lambda b, h, t: (0, 0)),
        ],
        out_specs=pl.BlockSpec((tb, dv), lambda b, h, t: (rowmap(b, h, t), h)),
        out_shape=jax.ShapeDtypeStruct((n, d_model), BF16),
        scratch_shapes=[pltpu.VMEM((dv, dk), F32)],
        compiler_params=_params(("parallel", "parallel", "arbitrary")),
        name="gla",
    )(proj, proj, proj, proj, proj, wa2p, ba, onorm)


def _shift_mix(z, prev_ref, mu, first_row):
    zs = pltpu.roll(z, 1, 0)
    zs = jnp.where(first_row, prev_ref[0:1, :], zs)
    prev_ref[0:1, :] = z[z.shape[0] - 1:z.shape[0], :]
    return z + (zs - z) * mu


def _rwkv_kernel(r_ref, k_ref, v_ref, sm_ref, mur_ref, muk_ref, muv_ref, musm_ref,
                 ww2_ref, wa2_ref, wg2_ref, w0_ref, a0_ref, kk_ref, ka_ref, rk_ref, lnw_ref, lnb_ref,
                 o_ref, ht_ref, pr_ref, pk_ref, pv_ref, psm_ref):
    @pl.when(pl.program_id(2) == 0)
    def _():
        ht_ref[...] = jnp.zeros_like(ht_ref)
        pr_ref[...] = jnp.zeros_like(pr_ref)
        pk_ref[...] = jnp.zeros_like(pk_ref)
        pv_ref[...] = jnp.zeros_like(pv_ref)
        psm_ref[...] = jnp.zeros_like(psm_ref)

    tb, gw = r_ref.shape
    hd = RWKV_HEAD
    first_row = lax.broadcasted_iota(jnp.int32, (tb, 1), 0) == 0
    r = _shift_mix(r_ref[...].astype(F32), pr_ref, mur_ref[...], first_row)
    k = _shift_mix(k_ref[...].astype(F32), pk_ref, muk_ref[...], first_row)
    v = _shift_mix(v_ref[...].astype(F32), pv_ref, muv_ref[...], first_row)
    sm = _shift_mix(sm_ref[...].astype(F32), psm_ref, musm_ref[...], first_row)
    w_lr = sm[:, 0:LANES]
    a_lr = sm[:, LANES:2 * LANES]
    g_lr = sm[:, 2 * LANES:]

    w_log = -_softplus(-(w0_ref[...] + _bdot(jnp.tanh(w_lr), ww2_ref[...]))) - 0.5
    lw = -jnp.exp(w_log)
    icl = jax.nn.sigmoid(a0_ref[...] + _bdot(a_lr, wa2_ref[...]))
    g = _bdot(jax.nn.sigmoid(g_lr), wg2_ref[...])

    lane_r = lax.broadcasted_iota(jnp.int32, (gw, gw), 0) // hd
    lane_c = lax.broadcasted_iota(jnp.int32, (gw, gw), 1) // hd
    same_head = lane_r == lane_c
    head_sum = same_head.astype(BF16)
    head_sum2 = jnp.concatenate([head_sum, head_sum], axis=0)

    def hsum(x):
        return jnp.dot(jnp.concatenate(_bf16_pieces(x, 2), axis=1), head_sum2, preferred_element_type=F32)

    kk = k * kk_ref[...]
    kk = kk / jnp.maximum(jnp.sqrt(hsum(kk * kk)), 1e-12)
    k2 = k * (1.0 + (icl - 1.0) * ka_ref[...])
    a = -kk
    b = kk * icl
    bonus = hsum(r * k2 * rk_ref[...]) * v

    head0 = lax.broadcasted_iota(jnp.int32, (1, gw), 1) < hd
    rr = lax.broadcasted_iota(jnp.int32, (gw, gw), 0) % CHUNK
    cc = lax.broadcasted_iota(jnp.int32, (gw, gw), 1) % CHUNK
    strict = rr > cc
    incl = rr >= cc
    eye = (lax.broadcasted_iota(jnp.int32, (gw, gw), 0)
           == lax.broadcasted_iota(jnp.int32, (gw, gw), 1)).astype(F32)
    tri3 = _tri3(CHUNK)

    def split_heads(z):
        return jnp.concatenate([jnp.where(head0, z, 0.0), jnp.where(head0, 0.0, z)], axis=0)

    nc = tb // CHUNK
    rows = [slice(c * CHUNK, (c + 1) * CHUNK) for c in range(nc)]
    cums = [_cumsum_rows(tri3, lw[sl]) for sl in rows]
    a_st, r_st, v_st, bh_st, kh_st, lhs, rhs, decay = [], [], [], [], [], [], [], []
    for sl, cum in zip(rows, cums):
        tot = cum[CHUNK - 1:CHUNK, :]
        e_neg = jnp.exp(-cum)
        e_rem = jnp.exp(tot - cum)
        a_st.append(split_heads(a[sl] * jnp.exp(cum - lw[sl])))
        r_st.append(split_heads(r[sl] * jnp.exp(cum)))
        v_st.append(split_heads(v[sl]))
        bh_st.append(split_heads(b[sl] * e_rem).astype(BF16))
        kh_st.append(split_heads(k2[sl] * e_rem).astype(BF16))
        lhs.append(jnp.concatenate([a_st[-1], r_st[-1]], axis=0).astype(BF16))
        rhs.append(jnp.concatenate([split_heads(b[sl] * e_neg), split_heads(k2[sl] * e_neg)],
                                   axis=0).astype(BF16))
        decay.append(jnp.exp(tot))
    pms = [lax.dot_general(lh, rh, NT, preferred_element_type=F32) for lh, rh in zip(lhs, rhs)]
    a_ab = [jnp.where(strict, pm[0:gw, 0:gw], 0.0) for pm in pms]
    a_ak = [jnp.where(strict, pm[0:gw, gw:], 0.0) for pm in pms]
    a_rb = [jnp.where(incl, pm[gw:, 0:gw], 0.0) for pm in pms]
    a_rk = [jnp.where(incl, pm[gw:, gw:], 0.0) for pm in pms]

    xs = a_ab
    t_inv = [eye + m for m in a_ab]
    for _ in range(5):
        xs = [_bdot(x, x) for x in xs]
        t_inv = [t + _bdot(t, x) for t, x in zip(t_inv, xs)]

    akv = [_bdot(m, vs) for m, vs in zip(a_ak, v_st)]
    zs = [_bdot(t, jnp.concatenate([as_, kv], axis=1)) for t, as_, kv in zip(t_inv, a_st, akv)]
    q_eff = [(rs + _bdot(m, z[:, 0:gw])).astype(BF16) for rs, m, z in zip(r_st, a_rb, zs)]
    y0 = [_bdot(jnp.concatenate([mb, mk], axis=1), jnp.concatenate([z[:, gw:], vs], axis=0))
          for mb, mk, z, vs in zip(a_rb, a_rk, zs, v_st)]
    v_t = [jnp.transpose(vs) for vs in v_st]
    z_t = [jnp.transpose(z) for z in zs]
    mzs = [_bdot(zt, bh) for zt, bh in zip(z_t, bh_st)]
    m_corr = [mz[0:gw].astype(BF16) for mz in mzs]
    s_add = [mz[gw:] + _bdot(vt, kh) for mz, vt, kh in zip(mzs, v_t, kh_st)]
    chunks = list(zip(q_eff, y0, m_corr, s_add, decay))

    ys = []
    ht = ht_ref[...]
    for q_eff, y0, m_corr, s_add, decay in chunks:
        hb = ht.astype(BF16)
        y_st = lax.dot_general(q_eff, hb, NT, preferred_element_type=F32) + y0
        ys.append(y_st[0:CHUNK] + y_st[CHUNK:])
        ht = ht * decay + jnp.dot(hb, m_corr, preferred_element_type=F32) + s_add
    ht_ref[...] = ht

    y = jnp.concatenate(ys, axis=0)
    mean = hsum(y) * (1.0 / hd)
    dlt = y - mean
    var = hsum(dlt * dlt) * (1.0 / hd)
    y = dlt * lax.rsqrt(var + RWKV_LN_EPS) * lnw_ref[...] + lnb_ref[...]
    o_ref[...] = ((y + bonus) * g).astype(o_ref.dtype)


def _rwkv(proj, mu_r, mu_k, mu_v, mu_sm, ww2p, wa2p, wg2, w0, a0, kkp, kap, rkp, lnw, lnb,
          *, batch, seq, d_model, tb, col_r, col_small):
    n = batch * seq
    nt = seq // tb
    gw = LANES
    ng = d_model // gw
    smw = mu_sm.shape[1]
    rowmap = lambda b, g, t: b * nt + t
    seg = lambda off: (lambda b, g, t: (rowmap(b, g, t), off // gw + g))
    vec = pl.BlockSpec((1, gw), lambda b, g, t: (0, g))
    return pl.pallas_call(
        _rwkv_kernel,
        grid=(batch, ng, nt),
        in_specs=[
            pl.BlockSpec((tb, gw), seg(col_r)),
            pl.BlockSpec((tb, gw), seg(col_r + d_model)),
            pl.BlockSpec((tb, gw), seg(col_r + 2 * d_model)),
            pl.BlockSpec((tb, smw), lambda b, g, t: (rowmap(b, g, t), col_small // smw)),
            vec, vec, vec,
            pl.BlockSpec((1, smw), lambda b, g, t: (0, 0)),
            pl.BlockSpec((LANES, gw), lambda b, g, t: (0, g)),
            pl.BlockSpec((LANES, gw), lambda b, g, t: (0, g)),
            pl.BlockSpec((wg2.shape[0], gw), lambda b, g, t: (0, g)),
            vec, vec, vec, vec, vec, vec, vec,
        ],
        out_specs=pl.BlockSpec((tb, gw), lambda b, g, t: (rowmap(b, g, t), g)),
        out_shape=jax.ShapeDtypeStruct((n, d_model), BF16),
        scratch_shapes=[
            pltpu.VMEM((gw, gw), F32),
            pltpu.VMEM((8, gw), F32), pltpu.VMEM((8, gw), F32), pltpu.VMEM((8, gw), F32),
            pltpu.VMEM((8, smw), F32),
        ],
        compiler_params=_params(("parallel", "parallel", "arbitrary")),
        name="rwkv",
    )(proj, proj, proj, proj, mu_r, mu_k, mu_v, mu_sm, ww2p, wa2p, wg2, w0, a0, kkp, kap, rkp, lnw, lnb)


def _merge_kernel(x_ref, oa_ref, ob_ref, ga_ref, gb_ref, wo_ref, g_ref, wr_ref, x1_ref, h2_ref, lg_ref):
    mixed = (jax.nn.sigmoid(ga_ref[...].astype(F32)) * oa_ref[...].astype(F32)
             + jax.nn.sigmoid(gb_ref[...].astype(F32)) * ob_ref[...].astype(F32))
    x1 = x_ref[...] + jnp.dot(mixed.astype(BF16), wo_ref[...], preferred_element_type=F32)
    x1_ref[...] = x1
    ms = jnp.mean(x1 * x1, axis=-1, keepdims=True)
    h2 = x1 * lax.rsqrt(ms + RMS_EPS) * g_ref[...]
    h2_ref[...] = h2
    lg_ref[...] = lax.dot_general(wr_ref[...], h2, NT, preferred_element_type=F32, precision=HIGHEST)


def _merge(x2d, o_a, o_b, proj, w_o, gain, w_router_t, *, tm, col_mga):
    n, d = x2d.shape
    ne = w_router_t.shape[0]
    row = lambda i: (i, 0)
    return pl.pallas_call(
        _merge_kernel,
        grid=(n // tm,),
        in_specs=[
            pl.BlockSpec((tm, d), row),
            pl.BlockSpec((tm, d), row),
            pl.BlockSpec((tm, d), row),
            pl.BlockSpec((tm, d), lambda i: (i, col_mga // d)),
            pl.BlockSpec((tm, d), lambda i: (i, col_mga // d + 1)),
            pl.BlockSpec((d, d), lambda i: (0, 0)),
            pl.BlockSpec((1, d), lambda i: (0, 0)),
            pl.BlockSpec((ne, d), lambda i: (0, 0)),
        ],
        out_specs=[
            pl.BlockSpec((tm, d), row),
            pl.BlockSpec((tm, d), row),
            pl.BlockSpec((ne, tm), lambda i: (0, i)),
        ],
        out_shape=[
            jax.ShapeDtypeStruct((n, d), F32),
            jax.ShapeDtypeStruct((n, d), F32),
            jax.ShapeDtypeStruct((ne, n), F32),
        ],
        compiler_params=_params(("parallel",)),
        name="merge",
    )(x2d, o_a, o_b, proj, proj, w_o, gain, w_router_t)


def _route_kernel(lg_ref, bias_ref, eidx_ref, gate_ref):
    ne, tr = lg_ref.shape
    gsz = ne // N_GROUPS
    neg = -jnp.inf
    scores = jax.nn.sigmoid(lg_ref[...])
    biased = scores + bias_ref[...]

    def first_max(vals, iota, size):
        m = jnp.max(vals, axis=0, keepdims=True)
        return m, jnp.min(jnp.where(vals == m, iota, float(size)), axis=0, keepdims=True)

    in_iota = lax.broadcasted_iota(jnp.int32, (gsz, tr), 0).astype(F32)
    slabs, gs_rows = [], []
    for grp in range(N_GROUPS):
        slab = biased[grp * gsz:(grp + 1) * gsz, :]
        m1, i1 = first_max(slab, in_iota, gsz)
        m2 = jnp.max(jnp.where(in_iota == i1, neg, slab), axis=0, keepdims=True)
        slabs.append(slab)
        gs_rows.append(m1 + m2)
    gs = jnp.concatenate(gs_rows, axis=0)
    g_iota = lax.broadcasted_iota(jnp.int32, (N_GROUPS, tr), 0).astype(F32)
    gsel = jnp.zeros((N_GROUPS, tr), F32)
    for _ in range(TOPK_GROUPS):
        _, gi = first_max(gs, g_iota, N_GROUPS)
        hit = g_iota == gi
        gsel = jnp.where(hit, 1.0, gsel)
        gs = jnp.where(hit, neg, gs)
    cand = jnp.concatenate(
        [jnp.where(gsel[grp:grp + 1, :] > 0.0, slabs[grp], neg) for grp in range(N_GROUPS)], axis=0)
    e_iota = lax.broadcasted_iota(jnp.int32, (ne, tr), 0).astype(F32)
    idxs, tops = [], []
    for _ in range(TOP_K):
        _, ei = first_max(cand, e_iota, ne)
        hit = e_iota == ei
        idxs.append(ei)
        tops.append(jnp.sum(jnp.where(hit, scores, 0.0), axis=0, keepdims=True))
        cand = jnp.where(hit, neg, cand)
    top_s = jnp.concatenate(tops, axis=0)
    eidx_ref[...] = jnp.concatenate(idxs, axis=0).astype(jnp.int32)
    gate_ref[...] = top_s / jnp.sum(top_s, axis=0, keepdims=True) * ROUTED_SCALE


def _route(logits_t, bias_col, *, tr):
    ne, n = logits_t.shape
    return pl.pallas_call(
        _route_kernel,
        grid=(n // tr,),
        in_specs=[pl.BlockSpec((ne, tr), lambda i: (0, i)), pl.BlockSpec((ne, 1), lambda i: (0, 0))],
        out_specs=[pl.BlockSpec((TOP_K, tr), lambda i: (0, i)), pl.BlockSpec((TOP_K, tr), lambda i: (0, i))],
        out_shape=[jax.ShapeDtypeStruct((TOP_K, n), jnp.int32), jax.ShapeDtypeStruct((TOP_K, n), F32)],
        compiler_params=_params(("parallel",)),
        name="route",
    )(logits_t, bias_col)


def _dispatch_tables(eidx, n_tokens):
    n = n_tokens
    nk = n * TOP_K
    n_blocks = -(-nk // MOE_BLOCK) + N_EXPERTS
    tok = jnp.broadcast_to(jnp.arange(n, dtype=jnp.int32)[None, :], (TOP_K, n))
    krow = jnp.arange(TOP_K, dtype=jnp.int32)[:, None]
    real_key = (eidx * (2 * n) + tok).reshape(nk)
    real_dest = (krow * n + tok).reshape(nk)
    counts = jnp.sum(eidx.reshape(1, nk) == jnp.arange(N_EXPERTS, dtype=jnp.int32)[:, None], axis=1)
    padded = (counts + MOE_BLOCK - 1) // MOE_BLOCK * MOE_BLOCK
    n_pad = n_blocks * MOE_BLOCK - nk
    pad_e = jnp.arange(n_pad, dtype=jnp.int32) // MOE_BLOCK
    pad_i = jnp.arange(n_pad, dtype=jnp.int32) % MOE_BLOCK
    pad_used = pad_i < (padded - counts)[jnp.minimum(pad_e, N_EXPERTS - 1)]
    pad_used = jnp.logical_and(pad_used, pad_e < N_EXPERTS)
    pad_key = jnp.where(pad_used, pad_e * (2 * n) + n + pad_i, jnp.iinfo(jnp.int32).max)
    pad_dest = nk + pad_i
    keys, dests = lax.sort_key_val(jnp.concatenate([real_key, pad_key]), jnp.concatenate([real_dest, pad_dest]))
    toks = jnp.minimum(keys % (2 * n), n - 1)
    pend_blocks = jnp.cumsum(padded) // MOE_BLOCK
    block_e = jnp.minimum(
        jnp.sum(pend_blocks[None, :] <= jnp.arange(n_blocks, dtype=jnp.int32)[:, None], axis=1), N_EXPERTS - 1)
    n_valid = pend_blocks[-1:]
    return (toks.reshape(n_blocks, 1, MOE_BLOCK), dests.reshape(n_blocks, 1, MOE_BLOCK),
            block_e.astype(jnp.int32), n_valid.astype(jnp.int32))


N_SLOTS = 3


def _experts_kernel(be_ref, nv_ref, tok0_ref, tok1_ref, tok2_ref, dest_ref, h_hbm, wg_ref, wu_ref, wd_ref,
                    out_hbm, xbuf, ybuf, wgb, wub, wdb, gsem, ssem, *, n_tokens):
    j = pl.program_id(0)
    nv = nv_ref[0]
    slot = j % N_SLOTS

    def gather_start(toks, s):
        for r in range(MOE_BLOCK):
            pltpu.make_async_copy(h_hbm.at[pl.ds(toks[0, r], 1), :], xbuf.at[s, pl.ds(r, 1), :],
                                  gsem.at[s]).start(priority=r % 2)

    def scatter_start(s):
        for r in range(MOE_BLOCK):
            pltpu.make_async_copy(ybuf.at[s, pl.ds(r, 1), :], out_hbm.at[pl.ds(dest_ref[0, r], 1), :],
                                  ssem.at[s]).start(priority=r % 2)

    def gather_wait(s):
        pltpu.make_async_copy(h_hbm.at[pl.ds(0, MOE_BLOCK), :], xbuf.at[s], gsem.at[s]).wait()

    def scatter_wait(s):
        pltpu.make_async_copy(ybuf.at[s], out_hbm.at[pl.ds(0, MOE_BLOCK), :], ssem.at[s]).wait()

    def block_step(s):
        gather_wait(s)
        gather_start(tok2_ref, (s + 2) % N_SLOTS)
        xb = xbuf[s].astype(BF16)
        hg = jnp.dot(xb, wgb[...], preferred_element_type=F32)
        hu = jnp.dot(xb, wub[...], preferred_element_type=F32)
        hb = (hg * jax.nn.sigmoid(hg) * hu).astype(BF16)
        ybuf[s] = jnp.dot(hb, wdb[...], preferred_element_type=F32)
        scatter_start(s)

    @pl.when(j == 0)
    def _():
        ybuf[1] = jnp.zeros(ybuf.shape[1:], ybuf.dtype)
        spare = pltpu.make_async_copy(ybuf.at[1], out_hbm.at[pl.ds(TOP_K * n_tokens, MOE_BLOCK), :], ssem.at[1])
        spare.start()
        spare.wait()

    @pl.when(j < nv)
    def _():
        @pl.when(j == 0)
        def _():
            gather_start(tok0_ref, 0)
            gather_start(tok1_ref, 1)

        @pl.when(jnp.logical_or(j == 0, be_ref[j] != be_ref[jnp.maximum(j - 1, 0)]))
        def _():
            wgb[...] = wg_ref[...].astype(BF16)
            wub[...] = wu_ref[...].astype(BF16)
            wdb[...] = wd_ref[...].astype(BF16)

        @pl.when(j >= N_SLOTS)
        def _():
            scatter_wait(slot)

        for s in range(N_SLOTS):
            pl.when(slot == s)(functools.partial(block_step, s))

        @pl.when(j == nv - 1)
        def _():
            gather_wait((j + 1) % N_SLOTS)
            gather_wait((j + 2) % N_SLOTS)
            scatter_wait(slot)
            for back in range(1, N_SLOTS):
                pl.when(j >= back)(functools.partial(scatter_wait, (j + N_SLOTS - back) % N_SLOTS))


def _experts(h2, toks, dests, block_e, n_valid, wg, wu, wd):
    n, d = h2.shape
    n_blocks = toks.shape[0]
    ff = wg.shape[2]

    def smem_blk(ahead):
        return pl.BlockSpec((None, 1, MOE_BLOCK), lambda j, be, nv: (jnp.minimum(j + ahead, n_blocks - 1), 0, 0),
                            memory_space=pltpu.SMEM)

    grid_spec = pltpu.PrefetchScalarGridSpec(
        num_scalar_prefetch=2,
        grid=(n_blocks,),
        in_specs=[
            smem_blk(0), smem_blk(1), smem_blk(2), smem_blk(0),
            pl.BlockSpec(memory_space=pl.ANY),
            pl.BlockSpec((None, d, ff), lambda j, be, nv: (be[j], 0, 0)),
            pl.BlockSpec((None, d, ff), lambda j, be, nv: (be[j], 0, 0)),
            pl.BlockSpec((None, ff, d), lambda j, be, nv: (be[j], 0, 0)),
        ],
        out_specs=pl.BlockSpec(memory_space=pl.ANY),
        scratch_shapes=[
            pltpu.VMEM((N_SLOTS, MOE_BLOCK, d), F32),
            pltpu.VMEM((N_SLOTS, MOE_BLOCK, d), F32),
            pltpu.VMEM((d, ff), BF16),
            pltpu.VMEM((d, ff), BF16),
            pltpu.VMEM((ff, d), BF16),
            pltpu.SemaphoreType.DMA((N_SLOTS,)),
            pltpu.SemaphoreType.DMA((N_SLOTS,)),
        ],
    )
    return pl.pallas_call(
        functools.partial(_experts_kernel, n_tokens=n),
        grid_spec=grid_spec,
        out_shape=jax.ShapeDtypeStruct((TOP_K * n + MOE_BLOCK, d), F32),
        compiler_params=_params(("arbitrary",)),
        name="experts",
    )(block_e, n_valid, toks, toks, toks, dests, h2, wg, wu, wd)


def _combine_kernel(x1_ref, h2_ref, gate_ref, *rest):
    y_refs = rest[:TOP_K]
    wg_ref, wu_ref, wd_ref, o_ref = rest[TOP_K:]
    gates = jnp.transpose(gate_ref[...])
    acc = x1_ref[...]
    for kk in range(TOP_K):
        acc = acc + gates[:, kk:kk + 1] * y_refs[kk][...]
    hb = h2_ref[...].astype(BF16)
    hg = jnp.dot(hb, wg_ref[...], preferred_element_type=F32)
    hu = jnp.dot(hb, wu_ref[...], preferred_element_type=F32)
    sh = jnp.dot((hg * jax.nn.sigmoid(hg) * hu).astype(BF16), wd_ref[...], preferred_element_type=F32)
    o_ref[...] = acc + sh


def _combine(x1, h2, gate8, y8, wsg, wsu, wsd, *, tm):
    n, d = x1.shape
    ff = wsg.shape[1]
    nb = n // tm
    row = lambda i: (i, 0)
    y_specs = [pl.BlockSpec((tm, d), functools.partial(lambda i, kk: (kk * nb + i, 0), kk=kk))
               for kk in range(TOP_K)]
    return pl.pallas_call(
        _combine_kernel,
        grid=(nb,),
        in_specs=[pl.BlockSpec((tm, d), row), pl.BlockSpec((tm, d), row),
                  pl.BlockSpec((TOP_K, tm), lambda i: (0, i))] + y_specs + [
                  pl.BlockSpec((d, ff), lambda i: (0, 0)), pl.BlockSpec((d, ff), lambda i: (0, 0)),
                  pl.BlockSpec((ff, d), lambda i: (0, 0))],
        out_specs=pl.BlockSpec((tm, d), row),
        out_shape=jax.ShapeDtypeStruct((n, d), F32),
        compiler_params=_params(("parallel",)),
        name="combine",
    )(x1, h2, gate8, *([y8] * TOP_K), wsg, wsu, wsd)


def _ple_kernel(x_ref, p_ref, gp_ref, wg_ref, wp_ref, *rest):
    o_ref = rest[-1]
    x = x_ref[...]
    ms = jnp.mean(x * x, axis=-1, keepdims=True)
    hn = (x * lax.rsqrt(ms + RMS_EPS) * gp_ref[...]).astype(BF16)
    gate = jax.nn.sigmoid(jnp.dot(hn, wg_ref[...], preferred_element_type=F32))
    emb = jnp.dot(p_ref[...].astype(BF16), wp_ref[...], preferred_element_type=F32)
    x = x + gate * emb
    if len(rest) == 2:
        ms = jnp.mean(x * x, axis=-1, keepdims=True)
        x = x * lax.rsqrt(ms + RMS_EPS) * rest[0][...]
    o_ref[...] = x


def _ple(x2, p2d, g_ple, w_pg, w_pp, g_final, *, tm):
    n, d = x2.shape
    pd = p2d.shape[1]
    row = lambda i: (i, 0)
    fix = lambda i: (0, 0)
    final = [] if g_final is None else [g_final]
    return pl.pallas_call(
        _ple_kernel,
        grid=(n // tm,),
        in_specs=[pl.BlockSpec((tm, d), row), pl.BlockSpec((tm, pd), row), pl.BlockSpec((1, d), fix),
                  pl.BlockSpec((d, d), fix), pl.BlockSpec((pd, d), fix)]
                 + [pl.BlockSpec((1, d), fix)] * len(final),
        out_specs=pl.BlockSpec((tm, d), row),
        out_shape=jax.ShapeDtypeStruct((n, d), F32),
        compiler_params=_params(("parallel",)),
        name="ple",
    )(x2, p2d, g_ple, w_pg, w_pp, *final)


def _pad_cols(w, width):
    return jnp.pad(w, ((0, 0), (0, width - w.shape[1])))


def _pad_rows(w, height):
    return jnp.pad(w, ((0, height - w.shape[0]), (0, 0)))


def _layer(x2d, p2d, lw, *, batch, seq, tiles):
    n, d = x2d.shape
    dk_all = d // 2
    w_in = lw['w_in']
    decay_rank = lw['rwkv_ww2'].shape[0]
    a_rank = lw['rwkv_wa2'].shape[0]
    g_rank = lw['rwkv_wg2'].shape[0]
    rw_width = 3 * d + decay_rank + a_rank + g_rank
    sizes = (dk_all, dk_all, d, GLA_GATE_RANK, d, rw_width, d, d)
    offs = [0]
    for s in sizes:
        offs.append(offs[-1] + s)
    seg = lambda i: w_in[:, offs[i]:offs[i + 1]]
    w_q, w_k, w_v, w_glr, w_og, w_rw, w_mga, w_mgb = (seg(i) for i in range(8))
    o_wlr, o_alr, o_glr = 3 * d, 3 * d + decay_rank, 3 * d + decay_rank + a_rank
    small = jnp.concatenate([
        _pad_cols(w_rw[:, o_wlr:o_alr], LANES), _pad_cols(w_rw[:, o_alr:o_glr], LANES), w_rw[:, o_glr:],
        _pad_cols(w_glr, LANES)], axis=1)
    small_w = 2 * LANES + g_rank
    tn = tiles['in_tn']
    small = _pad_cols(small, -(-small.shape[1] // tn) * tn)
    col_mga = 3 * d
    col_r = 5 * d
    col_small = 8 * d
    w_cat = jnp.concatenate([w_q, w_k, w_v, w_og, w_mga, w_mgb, w_rw[:, :3 * d], small], axis=1).astype(BF16)

    proj = _in_proj(x2d, lw['norm_mix'][None, :], w_cat, tiles['in_tm'], tn)

    o_a = _gla(proj, _pad_rows(lw['gla_wa2'], LANES).astype(BF16), lw['gla_ba'][None, :],
               lw['gla_onorm'][None, :], batch=batch, seq=seq, d_model=d, tb=tiles['gla_tb'],
               col_small_glr=col_small + small_w)

    mu = lw['mu_shift']
    mu_sm = jnp.concatenate([_pad_cols(mu[None, o_wlr:o_alr], LANES), _pad_cols(mu[None, o_alr:o_glr], LANES),
                             mu[None, o_glr:]], axis=1)
    rowv = lambda v: v.reshape(1, -1)
    o_b = _rwkv(proj, mu[None, 0:d], mu[None, d:2 * d], mu[None, 2 * d:3 * d], mu_sm,
                _pad_rows(lw['rwkv_ww2'], LANES).astype(BF16), _pad_rows(lw['rwkv_wa2'], LANES).astype(BF16),
                lw['rwkv_wg2'].astype(BF16), rowv(lw['rwkv_w0']), rowv(lw['rwkv_a0']), rowv(lw['rwkv_kk']),
                rowv(lw['rwkv_ka']), rowv(lw['rwkv_rk']), rowv(lw['rwkv_lnw']), rowv(lw['rwkv_lnb']),
                batch=batch, seq=seq, d_model=d, tb=tiles['rwkv_tb'], col_r=col_r, col_small=col_small)

    x1, h2, logits_t = _merge(x2d, o_a, o_b, proj, lw['w_o'].astype(BF16), lw['norm_ffn'][None, :],
                              jnp.transpose(lw['w_router']), tm=tiles['merge_tm'], col_mga=col_mga)

    eidx, gate8 = _route(logits_t, lw['router_bias'][:, None], tr=tiles['route_tr'])
    toks, dests, block_e, n_valid = _dispatch_tables(eidx, n)
    y8 = _experts(h2, toks, dests, block_e, n_valid, lw['w_exp_gate'], lw['w_exp_up'], lw['w_exp_down'])
    x2 = _combine(x1, h2, gate8, y8, lw['w_sh_gate'].astype(BF16), lw['w_sh_up'].astype(BF16),
                  lw['w_sh_down'].astype(BF16), tm=tiles['combine_tm'])
    return x2


_TILES = dict(in_tm=1024, in_tn=1024, gla_tb=512, rwkv_tb=512, merge_tm=256, route_tr=512,
              combine_tm=128, ple_tm=512)


def _forward(x, p, weights, norm_final, tiles):
    batch, seq, d = x.shape
    n = batch * seq
    depth = p.shape[0]
    x2d = x.reshape(n, d)
    for i in range(depth):
        lw = {k: v[i] for k, v in weights.items()}
        x2 = _layer(x2d, p[i].reshape(n, -1), lw, batch=batch, seq=seq, tiles=tiles)
        last = norm_final[None, :] if i == depth - 1 else None
        x2d = _ple(x2, p[i].reshape(n, -1), lw['norm_ple'][None, :], lw['w_ple_gate'].astype(BF16),
                   lw['w_ple_proj'].astype(BF16), last, tm=tiles['ple_tm'])
    return x2d.reshape(batch, seq, d)


def kernel(x, p, norm_mix, w_in, mu_shift, gla_wa2, gla_ba, gla_onorm, rwkv_w0, rwkv_ww2, rwkv_a0, rwkv_wa2, rwkv_wg2, rwkv_kk, rwkv_ka, rwkv_rk, rwkv_lnw, rwkv_lnb, w_o, norm_ffn, w_router, router_bias, w_exp_gate, w_exp_up, w_exp_down, w_sh_gate, w_sh_up, w_sh_down, norm_ple, w_ple_gate, w_ple_proj, norm_final):
    weights = dict(norm_mix=norm_mix, w_in=w_in, mu_shift=mu_shift, gla_wa2=gla_wa2, gla_ba=gla_ba,
                   gla_onorm=gla_onorm, rwkv_w0=rwkv_w0, rwkv_ww2=rwkv_ww2, rwkv_a0=rwkv_a0, rwkv_wa2=rwkv_wa2,
                   rwkv_wg2=rwkv_wg2, rwkv_kk=rwkv_kk, rwkv_ka=rwkv_ka, rwkv_rk=rwkv_rk, rwkv_lnw=rwkv_lnw,
                   rwkv_lnb=rwkv_lnb, w_o=w_o, norm_ffn=norm_ffn, w_router=w_router, router_bias=router_bias,
                   w_exp_gate=w_exp_gate, w_exp_up=w_exp_up, w_exp_down=w_exp_down, w_sh_gate=w_sh_gate,
                   w_sh_up=w_sh_up, w_sh_down=w_sh_down, norm_ple=norm_ple, w_ple_gate=w_ple_gate,
                   w_ple_proj=w_ple_proj)
    return _forward(x, p, weights, norm_final, _TILES)
```

```python
import functools

import jax
import jax.numpy as jnp
from jax import lax
from jax.experimental import pallas as pl
from jax.experimental.pallas import tpu as pltpu

F32 = jnp.float32
BF16 = jnp.bfloat16
HIGHEST = lax.Precision.HIGHEST

RMS_EPS = 1e-6
GLA_HEADS = 4
GLA_GATE_RANK = 16
GLA_TAU = 16.0
GLA_NORM_EPS = 1e-5
RWKV_HEAD = 64
RWKV_LN_EPS = 64e-5
N_EXPERTS = 64
TOP_K = 8
N_GROUPS = 8
TOPK_GROUPS = 4
ROUTED_SCALE = 2.5
MOE_BLOCK = 256
CHUNK = 64
LANES = 128
VMEM_LIMIT = 56 * 1024 * 1024

NT = (((1,), (1,)), ((), ()))
TN = (((0,), (0,)), ((), ()))


def _params(sem, vmem=VMEM_LIMIT):
    return pltpu.CompilerParams(dimension_semantics=sem, vmem_limit_bytes=vmem)


def _bdot(a, b):
    return jnp.dot(a.astype(BF16), b.astype(BF16), preferred_element_type=F32)


def _bf16_pieces(x, n):
    pieces = []
    for _ in range(n):
        p = x.astype(BF16)
        pieces.append(p)
        x = x - p.astype(F32)
    return pieces


def _cumsum_rows(tri3, x):
    return jnp.dot(tri3, jnp.concatenate(_bf16_pieces(x, 3), axis=0), preferred_element_type=F32)


def _tri3(c):
    tri = (lax.broadcasted_iota(jnp.int32, (c, c), 0) >= lax.broadcasted_iota(jnp.int32, (c, c), 1)).astype(BF16)
    return jnp.concatenate([tri, tri, tri], axis=1)


def _pack_halves(x):
    w = x.shape[1] // 2
    bits = lax.bitcast_convert_type(x.astype(BF16).astype(F32), jnp.uint32)
    return (bits[:, :w] >> 16) | bits[:, w:]


def _unpack_halves(u):
    lo = lax.bitcast_convert_type(u << 16, F32)
    hi = lax.bitcast_convert_type(u & jnp.uint32(0xFFFF0000), F32)
    return lo, hi


def _softplus(y):
    return jnp.maximum(y, 0.0) + jnp.log1p(jnp.exp(-jnp.abs(y)))


def _in_proj_kernel(x_ref, g_ref, w_ref, o_ref, h_ref):
    @pl.when(pl.program_id(1) == 0)
    def _():
        x = x_ref[...]
        ms = jnp.mean(x * x, axis=-1, keepdims=True)
        h_ref[...] = (x * lax.rsqrt(ms + RMS_EPS) * g_ref[...]).astype(BF16)

    o_ref[...] = jnp.dot(h_ref[...], w_ref[...], preferred_element_type=F32).astype(o_ref.dtype)


def _in_proj(x2d, gain, w_cat, tm, tn):
    n, d = x2d.shape
    m = w_cat.shape[1]
    return pl.pallas_call(
        _in_proj_kernel,
        grid=(n // tm, m // tn),
        in_specs=[
            pl.BlockSpec((tm, d), lambda i, j: (i, 0)),
            pl.BlockSpec((1, d), lambda i, j: (0, 0)),
            pl.BlockSpec((d, tn), lambda i, j: (0, j)),
        ],
        out_specs=pl.BlockSpec((tm, tn), lambda i, j: (i, j)),
        out_shape=jax.ShapeDtypeStruct((n, m), BF16),
        scratch_shapes=[pltpu.VMEM((tm, d), BF16)],
        compiler_params=_params(("parallel", "arbitrary")),
        name="in_proj",
    )(x2d, gain, w_cat)


def _gla_kernel(q_ref, k_ref, v_ref, og_ref, glr_ref, wa2_ref, ba_ref, on_ref, o_ref, st_ref, *, dk):
    @pl.when(pl.program_id(2) == 0)
    def _():
        st_ref[...] = jnp.zeros_like(st_ref)

    tb = q_ref.shape[0]
    row = lax.broadcasted_iota(jnp.int32, (CHUNK, CHUNK), 0)
    col = lax.broadcasted_iota(jnp.int32, (CHUNK, CHUNK), 1)
    causal = row >= col
    tri3 = _tri3(CHUNK)
    scale = dk ** -0.5
    rows = [pl.ds(c * CHUNK, CHUNK) for c in range(tb // CHUNK)]
    la = jnp.dot(glr_ref[...], wa2_ref[...], preferred_element_type=F32) + ba_ref[...]
    la = (jnp.minimum(la, 0.0) - jnp.log1p(jnp.exp(-jnp.abs(la)))) / GLA_TAU
    bcums = [_cumsum_rows(tri3, la[c * CHUNK:(c + 1) * CHUNK]) for c in range(len(rows))]
    q_t, k_t, k_d, decay = [], [], [], []
    for sl, bcum in zip(rows, bcums):
        k = k_ref[sl, :].astype(F32)
        btot = bcum[CHUNK - 1:CHUNK, :]
        q_t.append((q_ref[sl, :].astype(F32) * scale * jnp.exp(bcum)).astype(BF16))
        k_t.append((k * jnp.exp(-bcum)).astype(BF16))
        k_d.append((k * jnp.exp(btot - bcum)).astype(BF16))
        decay.append(jnp.exp(btot))
    atts = [lax.dot_general(qt, kt, NT, preferred_element_type=F32) for qt, kt in zip(q_t, k_t)]
    atts = [jnp.where(causal, att, 0.0).astype(BF16) for att in atts]
    upds = [lax.dot_general(v_ref[sl, :], kd, TN, preferred_element_type=F32) for sl, kd in zip(rows, k_d)]
    states = [st_ref[...]]
    for dec, upd in zip(decay, upds):
        states.append(states[-1] * dec + upd)
    st_ref[...] = states[-1]
    outs = [jnp.dot(att, v_ref[sl, :], preferred_element_type=F32)
            + lax.dot_general(qt, st.astype(BF16), NT, preferred_element_type=F32)
            for att, sl, qt, st in zip(atts, rows, q_t, states)]
    for sl, o in zip(rows, outs):
        ms = jnp.mean(o * o, axis=-1, keepdims=True)
        o = o * lax.rsqrt(ms + GLA_NORM_EPS) * on_ref[...]
        og = og_ref[sl, :].astype(F32)
        o_ref[sl, :] = (o * (og * jax.nn.sigmoid(og))).astype(o_ref.dtype)


def _gla(proj, wa2p, ba, onorm, *, batch, seq, d_model, tb, col_small_glr):
    heads = GLA_HEADS
    dk = d_model // 2 // heads
    dv = d_model // heads
    n = batch * seq
    nt = seq // tb
    rowmap = lambda b, h, t: b * nt + t
    kcol = (d_model // 2) // dk
    vcol = d_model // dv
    ogcol = 2 * d_model // dv
    return pl.pallas_call(
        functools.partial(_gla_kernel, dk=dk),
        grid=(batch, heads, nt),
        in_specs=[
            pl.BlockSpec((tb, dk), lambda b, h, t: (rowmap(b, h, t), h)),
            pl.BlockSpec((tb, dk), lambda b, h, t: (rowmap(b, h, t), kcol + h)),
            pl.BlockSpec((tb, dv), lambda b, h, t: (rowmap(b, h, t), vcol + h)),
            pl.BlockSpec((tb, dv), lambda b, h, t: (rowmap(b, h, t), ogcol + h)),
            pl.BlockSpec((tb, LANES), lambda b, h, t: (rowmap(b, h, t), col_small_glr // LANES)),
            pl.BlockSpec((LANES, dk), lambda b, h, t: (0, h)),
            pl.BlockSpec((1, dk), lambda b, h, t: (0, h)),
            pl.BlockSpec((1, dv), lambda b, h, t: (0, 0)),
        ],
        out_specs=pl.BlockSpec((tb, dv), lambda b, h, t: (rowmap(b, h, t), h)),
        out_shape=jax.ShapeDtypeStruct((n, d_model), BF16),
        scratch_shapes=[pltpu.VMEM((dv, dk), F32)],
        compiler_params=_params(("parallel", "parallel", "arbitrary")),
        name="gla",
    )(proj, proj, proj, proj, proj, wa2p, ba, onorm)


def _shift_mix(z, prev_ref, mu, first_row):
    zs = pltpu.roll(z, 1, 0)
    zs = jnp.where(first_row, prev_ref[0:1, :], zs)
    prev_ref[0:1, :] = z[z.shape[0] - 1:z.shape[0], :]
    return z + (zs - z) * mu


def _rwkv_kernel(r_ref, k_ref, v_ref, sm_ref, mur_ref, muk_ref, muv_ref, musm_ref,
                 ww2_ref, wa2_ref, wg2_ref, w0_ref, a0_ref, kk_ref, ka_ref, rk_ref, lnw_ref, lnb_ref,
                 o_ref, ht_ref, pr_ref, pk_ref, pv_ref, psm_ref, qe_s, y0_s, mc_s, sa_s, dec_s, bon_s, g_s):
    t = pl.program_id(2)
    last = pl.num_programs(2) - 1
    tb, gw = r_ref.shape
    hd = RWKV_HEAD
    nc = tb // CHUNK

    @pl.when(t == 0)
    def _():
        for ref in (ht_ref, pr_ref, pk_ref, pv_ref, psm_ref, qe_s, y0_s, mc_s, sa_s, dec_s, bon_s, g_s):
            ref[...] = jnp.zeros_like(ref)

    lane_r = lax.broadcasted_iota(jnp.int32, (gw, gw), 0) // hd
    lane_c = lax.broadcasted_iota(jnp.int32, (gw, gw), 1) // hd
    head_sum = (lane_r == lane_c).astype(BF16)
    head_sum2 = jnp.concatenate([head_sum, head_sum], axis=0)

    def hsum(x):
        return jnp.dot(jnp.concatenate(_bf16_pieces(x, 2), axis=1), head_sum2, preferred_element_type=F32)

    def previous_block():
        state = dict(ht=ht_ref[...], ys=[], c=0)

        def advance(n):
            for c in range(state['c'], min(state['c'] + n, nc)):
                hb = state['ht'].astype(BF16)
                y_st = lax.dot_general(qe_s[c], hb, NT, preferred_element_type=F32) + y0_s[c]
                state['ys'].append(y_st[0:CHUNK] + y_st[CHUNK:])
                state['ht'] = (state['ht'] * dec_s[c, 0:1, :]
                               + jnp.dot(hb, mc_s[c], preferred_element_type=F32) + sa_s[c])
                state['c'] = c + 1

        def finish():
            advance(nc)
            ht_ref[...] = state['ht']
            y = jnp.concatenate(state['ys'], axis=0)
            mean = hsum(y) * (1.0 / hd)
            dlt = y - mean
            var = hsum(dlt * dlt) * (1.0 / hd)
            y = dlt * lax.rsqrt(var + RWKV_LN_EPS) * lnw_ref[...] + lnb_ref[...]
            o_ref[...] = ((y + bon_s[...]) * g_s[...]).astype(o_ref.dtype)

        return advance, finish

    @pl.when(t == last)
    def _():
        previous_block()[1]()

    @pl.when(t < last)
    def _():
        _rwkv_block(r_ref, k_ref, v_ref, sm_ref, mur_ref, muk_ref, muv_ref, musm_ref, ww2_ref, wa2_ref, wg2_ref,
                    w0_ref, a0_ref, kk_ref, ka_ref, rk_ref, pr_ref, pk_ref, pv_ref, psm_ref,
                    qe_s, y0_s, mc_s, sa_s, dec_s, bon_s, g_s, hsum, previous_block)


def _rwkv_block(r_ref, k_ref, v_ref, sm_ref, mur_ref, muk_ref, muv_ref, musm_ref, ww2_ref, wa2_ref, wg2_ref,
                w0_ref, a0_ref, kk_ref, ka_ref, rk_ref, pr_ref, pk_ref, pv_ref, psm_ref,
                qe_s, y0_s, mc_s, sa_s, dec_s, bon_s, g_s, hsum, previous_block):
    tb, gw = r_ref.shape
    hd = RWKV_HEAD
    nc = tb // CHUNK
    advance, finish = previous_block()
    per_stage = -(-nc // 8)

    first_row = lax.broadcasted_iota(jnp.int32, (tb, 1), 0) == 0
    r = _shift_mix(r_ref[...].astype(F32), pr_ref, mur_ref[...], first_row)
    k = _shift_mix(k_ref[...].astype(F32), pk_ref, muk_ref[...], first_row)
    v = _shift_mix(v_ref[...].astype(F32), pv_ref, muv_ref[...], first_row)
    sm = _shift_mix(sm_ref[...].astype(F32), psm_ref, musm_ref[...], first_row)
    w_lr = sm[:, 0:LANES]
    a_lr = sm[:, LANES:2 * LANES]
    g_lr = sm[:, 2 * LANES:]

    w_log = -_softplus(-(w0_ref[...] + _bdot(jnp.tanh(w_lr), ww2_ref[...]))) - 0.5
    lw = -jnp.exp(w_log)
    icl = jax.nn.sigmoid(a0_ref[...] + _bdot(a_lr, wa2_ref[...]))
    g = _bdot(jax.nn.sigmoid(g_lr), wg2_ref[...])

    kk = k * kk_ref[...]
    kk = kk / jnp.maximum(jnp.sqrt(hsum(kk * kk)), 1e-12)
    k2 = k * (1.0 + (icl - 1.0) * ka_ref[...])
    a = -kk
    b = kk * icl
    bonus = hsum(r * k2 * rk_ref[...]) * v

    head0 = lax.broadcasted_iota(jnp.int32, (1, gw), 1) < hd
    rr = lax.broadcasted_iota(jnp.int32, (gw, gw), 0) % CHUNK
    cc = lax.broadcasted_iota(jnp.int32, (gw, gw), 1) % CHUNK
    strict = rr > cc
    incl = rr >= cc
    eye = (lax.broadcasted_iota(jnp.int32, (gw, gw), 0)
           == lax.broadcasted_iota(jnp.int32, (gw, gw), 1)).astype(F32)
    tri3 = _tri3(CHUNK)

    def split_heads(z):
        return jnp.concatenate([jnp.where(head0, z, 0.0), jnp.where(head0, 0.0, z)], axis=0)

    rows = [slice(c * CHUNK, (c + 1) * CHUNK) for c in range(nc)]
    cums = [_cumsum_rows(tri3, lw[sl]) for sl in rows]
    a_st, r_st, v_st, bh_st, kh_st, lhs, rhs, decay = [], [], [], [], [], [], [], []
    for sl, cum in zip(rows, cums):
        tot = cum[CHUNK - 1:CHUNK, :]
        e_neg = jnp.exp(-cum)
        e_rem = jnp.exp(tot - cum)
        a_st.append(split_heads(a[sl] * jnp.exp(cum - lw[sl])))
        r_st.append(split_heads(r[sl] * jnp.exp(cum)))
        v_st.append(split_heads(v[sl]))
        bh_st.append(split_heads(b[sl] * e_rem).astype(BF16))
        kh_st.append(split_heads(k2[sl] * e_rem).astype(BF16))
        lhs.append(jnp.concatenate([a_st[-1], r_st[-1]], axis=0).astype(BF16))
        rhs.append(jnp.concatenate([split_heads(b[sl] * e_neg), split_heads(k2[sl] * e_neg)],
                                   axis=0).astype(BF16))
        decay.append(jnp.exp(tot))
    pms = [lax.dot_general(lh, rh, NT, preferred_element_type=F32) for lh, rh in zip(lhs, rhs)]
    advance(per_stage)
    a_ab = [jnp.where(strict, pm[0:gw, 0:gw], 0.0) for pm in pms]
    a_ak = [jnp.where(strict, pm[0:gw, gw:], 0.0) for pm in pms]
    a_rb = [jnp.where(incl, pm[gw:, 0:gw], 0.0) for pm in pms]
    a_rk = [jnp.where(incl, pm[gw:, gw:], 0.0) for pm in pms]

    xs = a_ab
    t_inv = [eye + m for m in a_ab]
    for _ in range(5):
        xs = [_bdot(x, x) for x in xs]
        t_inv = [t + _bdot(t, x) for t, x in zip(t_inv, xs)]
        advance(per_stage)

    akv = [_bdot(m, vs) for m, vs in zip(a_ak, v_st)]
    advance(per_stage)
    zs = [_bdot(t, jnp.concatenate([as_, kv], axis=1)) for t, as_, kv in zip(t_inv, a_st, akv)]
    advance(per_stage)
    q_eff = [(rs + _bdot(m, z[:, 0:gw])).astype(BF16) for rs, m, z in zip(r_st, a_rb, zs)]
    y0 = [_bdot(jnp.concatenate([mb, mk], axis=1), jnp.concatenate([z[:, gw:], vs], axis=0))
          for mb, mk, z, vs in zip(a_rb, a_rk, zs, v_st)]
    v_t = [jnp.transpose(vs) for vs in v_st]
    z_t = [jnp.transpose(z) for z in zs]
    mzs = [_bdot(zt, bh) for zt, bh in zip(z_t, bh_st)]
    m_corr = [mz[0:gw].astype(BF16) for mz in mzs]
    s_add = [mz[gw:] + _bdot(vt, kh) for mz, vt, kh in zip(mzs, v_t, kh_st)]

    finish()
    for c in range(nc):
        qe_s[c] = q_eff[c]
        y0_s[c] = y0[c]
        mc_s[c] = m_corr[c]
        sa_s[c] = s_add[c]
        dec_s[c, 0:1, :] = decay[c]
    bon_s[...] = bonus
    g_s[...] = g


def _rwkv(proj, mu_r, mu_k, mu_v, mu_sm, ww2p, wa2p, wg2, w0, a0, kkp, kap, rkp, lnw, lnb,
          *, batch, seq, d_model, tb, col_r, col_small):
    n = batch * seq
    nt = seq // tb
    gw = LANES
    ng = d_model // gw
    smw = mu_sm.shape[1]
    nc = tb // CHUNK
    rowmap = lambda b, g, t: b * nt + jnp.minimum(t, nt - 1)
    seg = lambda off: (lambda b, g, t: (rowmap(b, g, t), off // gw + g))
    vec = pl.BlockSpec((1, gw), lambda b, g, t: (0, g))
    return pl.pallas_call(
        _rwkv_kernel,
        grid=(batch, ng, nt + 1),
        in_specs=[
            pl.BlockSpec((tb, gw), seg(col_r)),
            pl.BlockSpec((tb, gw), seg(col_r + d_model)),
            pl.BlockSpec((tb, gw), seg(col_r + 2 * d_model)),
            pl.BlockSpec((tb, smw), lambda b, g, t: (rowmap(b, g, t), col_small // smw)),
            vec, vec, vec,
            pl.BlockSpec((1, smw), lambda b, g, t: (0, 0)),
            pl.BlockSpec((LANES, gw), lambda b, g, t: (0, g)),
            pl.BlockSpec((LANES, gw), lambda b, g, t: (0, g)),
            pl.BlockSpec((wg2.shape[0], gw), lambda b, g, t: (0, g)),
            vec, vec, vec, vec, vec, vec, vec,
        ],
        out_specs=pl.BlockSpec((tb, gw), lambda b, g, t: (b * nt + jnp.maximum(t - 1, 0), g)),
        out_shape=jax.ShapeDtypeStruct((n, d_model), BF16),
        scratch_shapes=[
            pltpu.VMEM((gw, gw), F32),
            pltpu.VMEM((8, gw), F32), pltpu.VMEM((8, gw), F32), pltpu.VMEM((8, gw), F32),
            pltpu.VMEM((8, smw), F32),
            pltpu.VMEM((nc, gw, gw), BF16), pltpu.VMEM((nc, gw, gw), F32),
            pltpu.VMEM((nc, gw, gw), BF16), pltpu.VMEM((nc, gw, gw), F32),
            pltpu.VMEM((nc, 8, gw), F32),
            pltpu.VMEM((tb, gw), F32), pltpu.VMEM((tb, gw), F32),
        ],
        compiler_params=_params(("parallel", "parallel", "arbitrary")),
        name="rwkv",
    )(proj, proj, proj, proj, mu_r, mu_k, mu_v, mu_sm, ww2p, wa2p, wg2, w0, a0, kkp, kap, rkp, lnw, lnb)


MERGE_PARTS = 2


def _merge_kernel(x_ref, oa_ref, ob_ref, ga_ref, gb_ref, wo_ref, g_ref, wr_ref, x1_ref, h2_ref, lg_ref):
    tm = x_ref.shape[0]
    parts = [pl.ds(i * (tm // MERGE_PARTS), tm // MERGE_PARTS) for i in range(MERGE_PARTS)]
    mixed = [(jax.nn.sigmoid(ga_ref[p, :].astype(F32)) * oa_ref[p, :].astype(F32)
              + jax.nn.sigmoid(gb_ref[p, :].astype(F32)) * ob_ref[p, :].astype(F32)).astype(BF16) for p in parts]
    x1s = [x_ref[p, :] + jnp.dot(m, wo_ref[...], preferred_element_type=F32) for p, m in zip(parts, mixed)]
    h2s = []
    for p, x1 in zip(parts, x1s):
        x1_ref[p, :] = x1
        ms = jnp.mean(x1 * x1, axis=-1, keepdims=True)
        h2 = x1 * lax.rsqrt(ms + RMS_EPS) * g_ref[...]
        h2_ref[p, :] = _pack_halves(h2)
        h2s.append(h2)
    lg_ref[...] = jnp.concatenate(
        [lax.dot_general(wr_ref[...], h2, NT, preferred_element_type=F32, precision=HIGHEST) for h2 in h2s], axis=1)


def _merge(x2d, o_a, o_b, proj, w_o, gain, w_router_t, *, tm, col_mga):
    n, d = x2d.shape
    ne = w_router_t.shape[0]
    row = lambda i: (i, 0)
    return pl.pallas_call(
        _merge_kernel,
        grid=(n // tm,),
        in_specs=[
            pl.BlockSpec((tm, d), row),
            pl.BlockSpec((tm, d), row),
            pl.BlockSpec((tm, d), row),
            pl.BlockSpec((tm, d), lambda i: (i, col_mga // d)),
            pl.BlockSpec((tm, d), lambda i: (i, col_mga // d + 1)),
            pl.BlockSpec((d, d), lambda i: (0, 0)),
            pl.BlockSpec((1, d), lambda i: (0, 0)),
            pl.BlockSpec((ne, d), lambda i: (0, 0)),
        ],
        out_specs=[
            pl.BlockSpec((tm, d), row),
            pl.BlockSpec((tm, d // 2), row),
            pl.BlockSpec((ne, tm), lambda i: (0, i)),
        ],
        out_shape=[
            jax.ShapeDtypeStruct((n, d), F32),
            jax.ShapeDtypeStruct((n, d // 2), jnp.uint32),
            jax.ShapeDtypeStruct((ne, n), F32),
        ],
        compiler_params=_params(("parallel",)),
        name="merge",
    )(x2d, o_a, o_b, proj, proj, w_o, gain, w_router_t)


def _route_kernel(lg_ref, bias_ref, eidx_ref, gate_ref):
    ne, tr = lg_ref.shape
    gsz = ne // N_GROUPS
    neg = -jnp.inf
    scores = jax.nn.sigmoid(lg_ref[...])
    biased = scores + bias_ref[...]

    def first_max(vals, iota, size):
        m = jnp.max(vals, axis=0, keepdims=True)
        return m, jnp.min(jnp.where(vals == m, iota, float(size)), axis=0, keepdims=True)

    in_iota = lax.broadcasted_iota(jnp.int32, (gsz, tr), 0).astype(F32)
    slabs, gs_rows = [], []
    for grp in range(N_GROUPS):
        slab = biased[grp * gsz:(grp + 1) * gsz, :]
        m1, i1 = first_max(slab, in_iota, gsz)
        m2 = jnp.max(jnp.where(in_iota == i1, neg, slab), axis=0, keepdims=True)
        slabs.append(slab)
        gs_rows.append(m1 + m2)
    gs = jnp.concatenate(gs_rows, axis=0)
    g_iota = lax.broadcasted_iota(jnp.int32, (N_GROUPS, tr), 0).astype(F32)
    gsel = jnp.zeros((N_GROUPS, tr), F32)
    for _ in range(TOPK_GROUPS):
        _, gi = first_max(gs, g_iota, N_GROUPS)
        hit = g_iota == gi
        gsel = jnp.where(hit, 1.0, gsel)
        gs = jnp.where(hit, neg, gs)
    cand = jnp.concatenate(
        [jnp.where(gsel[grp:grp + 1, :] > 0.0, slabs[grp], neg) for grp in range(N_GROUPS)], axis=0)
    e_iota = lax.broadcasted_iota(jnp.int32, (ne, tr), 0).astype(F32)
    idxs, tops = [], []
    for _ in range(TOP_K):
        _, ei = first_max(cand, e_iota, ne)
        hit = e_iota == ei
        idxs.append(ei)
        tops.append(jnp.sum(jnp.where(hit, scores, 0.0), axis=0, keepdims=True))
        cand = jnp.where(hit, neg, cand)
    top_s = jnp.concatenate(tops, axis=0)
    eidx_ref[...] = jnp.concatenate(idxs, axis=0).astype(jnp.int32)
    gate_ref[...] = top_s / jnp.sum(top_s, axis=0, keepdims=True) * ROUTED_SCALE


def _route(logits_t, bias_col, *, tr):
    ne, n = logits_t.shape
    return pl.pallas_call(
        _route_kernel,
        grid=(n // tr,),
        in_specs=[pl.BlockSpec((ne, tr), lambda i: (0, i)), pl.BlockSpec((ne, 1), lambda i: (0, 0))],
        out_specs=[pl.BlockSpec((TOP_K, tr), lambda i: (0, i)), pl.BlockSpec((TOP_K, tr), lambda i: (0, i))],
        out_shape=[jax.ShapeDtypeStruct((TOP_K, n), jnp.int32), jax.ShapeDtypeStruct((TOP_K, n), F32)],
        compiler_params=_params(("parallel",)),
        name="route",
    )(logits_t, bias_col)


def _dispatch_tables(eidx, n_tokens):
    n = n_tokens
    nk = n * TOP_K
    n_blocks = -(-nk // MOE_BLOCK) + N_EXPERTS
    tok = jnp.broadcast_to(jnp.arange(n, dtype=jnp.int32)[None, :], (TOP_K, n))
    krow = jnp.arange(TOP_K, dtype=jnp.int32)[:, None]
    real_key = (eidx * (2 * n) + tok).reshape(nk)
    real_dest = (krow * n + tok).reshape(nk)
    counts = jnp.sum(eidx.reshape(1, nk) == jnp.arange(N_EXPERTS, dtype=jnp.int32)[:, None], axis=1)
    padded = (counts + MOE_BLOCK - 1) // MOE_BLOCK * MOE_BLOCK
    n_pad = n_blocks * MOE_BLOCK - nk
    pad_e = jnp.arange(n_pad, dtype=jnp.int32) // MOE_BLOCK
    pad_i = jnp.arange(n_pad, dtype=jnp.int32) % MOE_BLOCK
    pad_used = pad_i < (padded - counts)[jnp.minimum(pad_e, N_EXPERTS - 1)]
    pad_used = jnp.logical_and(pad_used, pad_e < N_EXPERTS)
    pad_key = jnp.where(pad_used, pad_e * (2 * n) + n + pad_i, jnp.iinfo(jnp.int32).max)
    pad_dest = nk + pad_i
    keys, dests = lax.sort_key_val(jnp.concatenate([real_key, pad_key]), jnp.concatenate([real_dest, pad_dest]))
    toks = jnp.minimum(keys % (2 * n), n - 1)
    pend_blocks = jnp.cumsum(padded) // MOE_BLOCK
    block_e = jnp.minimum(
        jnp.sum(pend_blocks[None, :] <= jnp.arange(n_blocks, dtype=jnp.int32)[:, None], axis=1), N_EXPERTS - 1)
    n_valid = pend_blocks[-1:]
    return (toks.reshape(n_blocks, 1, MOE_BLOCK), dests.reshape(n_blocks, 1, MOE_BLOCK),
            block_e.astype(jnp.int32), n_valid.astype(jnp.int32))


N_SLOTS = 3


def _experts_kernel(be_ref, nv_ref, tok0_ref, tok1_ref, tok2_ref, dest_ref, h_hbm, wg_ref, wu_ref, wd_ref,
                    out_hbm, xbuf, ybuf, wgb, wub, wdb, gsem, ssem, *, n_tokens):
    j = pl.program_id(0)
    nv = nv_ref[0]
    slot = j % N_SLOTS

    def gather_start(toks, s):
        for r in range(MOE_BLOCK):
            pltpu.make_async_copy(h_hbm.at[pl.ds(toks[0, r], 1), :], xbuf.at[s, pl.ds(r, 1), :],
                                  gsem.at[s]).start(priority=r % 2)

    def scatter_start(s):
        for r in range(MOE_BLOCK):
            pltpu.make_async_copy(ybuf.at[s, pl.ds(r, 1), :], out_hbm.at[pl.ds(dest_ref[0, r], 1), :],
                                  ssem.at[s]).start(priority=r % 2)

    def gather_wait(s):
        pltpu.make_async_copy(h_hbm.at[pl.ds(0, MOE_BLOCK), :], xbuf.at[s], gsem.at[s]).wait()

    def scatter_wait(s):
        pltpu.make_async_copy(ybuf.at[s], out_hbm.at[pl.ds(0, MOE_BLOCK), :], ssem.at[s]).wait()

    def block_step(s):
        gather_wait(s)
        gather_start(tok2_ref, (s + 2) % N_SLOTS)
        half = wgb.shape[0] // 2
        x_lo, x_hi = (z.astype(BF16) for z in _unpack_halves(xbuf[s]))
        hg = (jnp.dot(x_lo, wgb[0:half, :], preferred_element_type=F32)
              + jnp.dot(x_hi, wgb[half:, :], preferred_element_type=F32))
        hu = (jnp.dot(x_lo, wub[0:half, :], preferred_element_type=F32)
              + jnp.dot(x_hi, wub[half:, :], preferred_element_type=F32))
        hb = (hg * jax.nn.sigmoid(hg) * hu).astype(BF16)
        ybuf[s] = _pack_halves(jnp.dot(hb, wdb[...], preferred_element_type=F32))
        scatter_start(s)

    @pl.when(j == 0)
    def _():
        ybuf[1] = jnp.zeros(ybuf.shape[1:], ybuf.dtype)
        spare = pltpu.make_async_copy(ybuf.at[1], out_hbm.at[pl.ds(TOP_K * n_tokens, MOE_BLOCK), :], ssem.at[1])
        spare.start()
        spare.wait()

    @pl.when(j < nv)
    def _():
        @pl.when(j == 0)
        def _():
            gather_start(tok0_ref, 0)
            gather_start(tok1_ref, 1)

        @pl.when(jnp.logical_or(j == 0, be_ref[j] != be_ref[jnp.maximum(j - 1, 0)]))
        def _():
            wgb[...] = wg_ref[...].astype(BF16)
            wub[...] = wu_ref[...].astype(BF16)
            wdb[...] = wd_ref[...].astype(BF16)

        @pl.when(j >= N_SLOTS)
        def _():
            scatter_wait(slot)

        for s in range(N_SLOTS):
            pl.when(slot == s)(functools.partial(block_step, s))

        @pl.when(j == nv - 1)
        def _():
            gather_wait((j + 1) % N_SLOTS)
            gather_wait((j + 2) % N_SLOTS)
            scatter_wait(slot)
            for back in range(1, N_SLOTS):
                pl.when(j >= back)(functools.partial(scatter_wait, (j + N_SLOTS - back) % N_SLOTS))


def _experts(h2p, toks, dests, block_e, n_valid, wg, wu, wd):
    n, dp = h2p.shape
    d = 2 * dp
    n_blocks = toks.shape[0]
    ff = wg.shape[2]

    def smem_blk(ahead):
        return pl.BlockSpec((None, 1, MOE_BLOCK), lambda j, be, nv: (jnp.minimum(j + ahead, n_blocks - 1), 0, 0),
                            memory_space=pltpu.SMEM)

    grid_spec = pltpu.PrefetchScalarGridSpec(
        num_scalar_prefetch=2,
        grid=(n_blocks,),
        in_specs=[
            smem_blk(0), smem_blk(1), smem_blk(2), smem_blk(0),
            pl.BlockSpec(memory_space=pl.ANY),
            pl.BlockSpec((None, d, ff), lambda j, be, nv: (be[j], 0, 0)),
            pl.BlockSpec((None, d, ff), lambda j, be, nv: (be[j], 0, 0)),
            pl.BlockSpec((None, ff, d), lambda j, be, nv: (be[j], 0, 0)),
        ],
        out_specs=pl.BlockSpec(memory_space=pl.ANY),
        scratch_shapes=[
            pltpu.VMEM((N_SLOTS, MOE_BLOCK, dp), jnp.uint32),
            pltpu.VMEM((N_SLOTS, MOE_BLOCK, dp), jnp.uint32),
            pltpu.VMEM((d, ff), BF16),
            pltpu.VMEM((d, ff), BF16),
            pltpu.VMEM((ff, d), BF16),
            pltpu.SemaphoreType.DMA((N_SLOTS,)),
            pltpu.SemaphoreType.DMA((N_SLOTS,)),
        ],
    )
    return pl.pallas_call(
        functools.partial(_experts_kernel, n_tokens=n),
        grid_spec=grid_spec,
        out_shape=jax.ShapeDtypeStruct((TOP_K * n + MOE_BLOCK, dp), jnp.uint32),
        compiler_params=_params(("arbitrary",)),
        name="experts",
    )(block_e, n_valid, toks, toks, toks, dests, h2p, wg, wu, wd)


def _combine_kernel(x1_ref, h2_ref, gate_ref, *rest):
    y_refs = rest[:TOP_K]
    wg_ref, wu_ref, wd_ref, o_ref = rest[TOP_K:]
    gates = jnp.transpose(gate_ref[...])
    half = x1_ref.shape[1] // 2
    acc_lo = x1_ref[:, 0:half]
    acc_hi = x1_ref[:, half:]
    for kk in range(TOP_K):
        y_lo, y_hi = _unpack_halves(y_refs[kk][...])
        acc_lo = acc_lo + gates[:, kk:kk + 1] * y_lo
        acc_hi = acc_hi + gates[:, kk:kk + 1] * y_hi
    h_lo, h_hi = (z.astype(BF16) for z in _unpack_halves(h2_ref[...]))
    hg = (jnp.dot(h_lo, wg_ref[0:half, :], preferred_element_type=F32)
          + jnp.dot(h_hi, wg_ref[half:, :], preferred_element_type=F32))
    hu = (jnp.dot(h_lo, wu_ref[0:half, :], preferred_element_type=F32)
          + jnp.dot(h_hi, wu_ref[half:, :], preferred_element_type=F32))
    sh = jnp.dot((hg * jax.nn.sigmoid(hg) * hu).astype(BF16), wd_ref[...], preferred_element_type=F32)
    o_ref[...] = jnp.concatenate([acc_lo, acc_hi], axis=1) + sh


def _combine(x1, h2p, gate8, y8, wsg, wsu, wsd, *, tm):
    n, d = x1.shape
    ff = wsg.shape[1]
    nb = n // tm
    row = lambda i: (i, 0)
    y_specs = [pl.BlockSpec((tm, d // 2), functools.partial(lambda i, kk: (kk * nb + i, 0), kk=kk))
               for kk in range(TOP_K)]
    return pl.pallas_call(
        _combine_kernel,
        grid=(nb,),
        in_specs=[pl.BlockSpec((tm, d), row), pl.BlockSpec((tm, d // 2), row),
                  pl.BlockSpec((TOP_K, tm), lambda i: (0, i))] + y_specs + [
                  pl.BlockSpec((d, ff), lambda i: (0, 0)), pl.BlockSpec((d, ff), lambda i: (0, 0)),
                  pl.BlockSpec((ff, d), lambda i: (0, 0))],
        out_specs=pl.BlockSpec((tm, d), row),
        out_shape=jax.ShapeDtypeStruct((n, d), F32),
        compiler_params=_params(("parallel",)),
        name="combine",
    )(x1, h2p, gate8, *([y8] * TOP_K), wsg, wsu, wsd)


def _ple_kernel(x_ref, p_ref, gp_ref, wg_ref, wp_ref, *rest):
    o_ref = rest[-1]
    x = x_ref[...]
    ms = jnp.mean(x * x, axis=-1, keepdims=True)
    hn = (x * lax.rsqrt(ms + RMS_EPS) * gp_ref[...]).astype(BF16)
    gate = jax.nn.sigmoid(jnp.dot(hn, wg_ref[...], preferred_element_type=F32))
    emb = jnp.dot(p_ref[...].astype(BF16), wp_ref[...], preferred_element_type=F32)
    x = x + gate * emb
    if len(rest) == 2:
        ms = jnp.mean(x * x, axis=-1, keepdims=True)
        x = x * lax.rsqrt(ms + RMS_EPS) * rest[0][...]
    o_ref[...] = x


def _ple(x2, p2d, g_ple, w_pg, w_pp, g_final, *, tm):
    n, d = x2.shape
    pd = p2d.shape[1]
    row = lambda i: (i, 0)
    fix = lambda i: (0, 0)
    final = [] if g_final is None else [g_final]
    return pl.pallas_call(
        _ple_kernel,
        grid=(n // tm,),
        in_specs=[pl.BlockSpec((tm, d), row), pl.BlockSpec((tm, pd), row), pl.BlockSpec((1, d), fix),
                  pl.BlockSpec((d, d), fix), pl.BlockSpec((pd, d), fix)]
                 + [pl.BlockSpec((1, d), fix)] * len(final),
        out_specs=pl.BlockSpec((tm, d), row),
        out_shape=jax.ShapeDtypeStruct((n, d), F32),
        compiler_params=_params(("parallel",)),
        name="ple",
    )(x2, p2d, g_ple, w_pg, w_pp, *final)


def _pad_cols(w, width):
    return jnp.pad(w, ((0, 0), (0, width - w.shape[1])))


def _pad_rows(w, height):
    return jnp.pad(w, ((0, height - w.shape[0]), (0, 0)))


def _layer(x2d, p2d, lw, *, batch, seq, tiles):
    n, d = x2d.shape
    dk_all = d // 2
    w_in = lw['w_in']
    decay_rank = lw['rwkv_ww2'].shape[0]
    a_rank = lw['rwkv_wa2'].shape[0]
    g_rank = lw['rwkv_wg2'].shape[0]
    rw_width = 3 * d + decay_rank + a_rank + g_rank
    sizes = (dk_all, dk_all, d, GLA_GATE_RANK, d, rw_width, d, d)
    offs = [0]
    for s in sizes:
        offs.append(offs[-1] + s)
    seg = lambda i: w_in[:, offs[i]:offs[i + 1]]
    w_q, w_k, w_v, w_glr, w_og, w_rw, w_mga, w_mgb = (seg(i) for i in range(8))
    o_wlr, o_alr, o_glr = 3 * d, 3 * d + decay_rank, 3 * d + decay_rank + a_rank
    small = jnp.concatenate([
        _pad_cols(w_rw[:, o_wlr:o_alr], LANES), _pad_cols(w_rw[:, o_alr:o_glr], LANES), w_rw[:, o_glr:],
        _pad_cols(w_glr, LANES)], axis=1)
    small_w = 2 * LANES + g_rank
    tn = tiles['in_tn']
    small = _pad_cols(small, -(-small.shape[1] // tn) * tn)
    col_mga = 3 * d
    col_r = 5 * d
    col_small = 8 * d
    w_cat = jnp.concatenate([w_q, w_k, w_v, w_og, w_mga, w_mgb, w_rw[:, :3 * d], small], axis=1).astype(BF16)

    proj = _in_proj(x2d, lw['norm_mix'][None, :], w_cat, tiles['in_tm'], tn)

    o_a = _gla(proj, _pad_rows(lw['gla_wa2'], LANES).astype(BF16), lw['gla_ba'][None, :],
               lw['gla_onorm'][None, :], batch=batch, seq=seq, d_model=d, tb=tiles['gla_tb'],
               col_small_glr=col_small + small_w)

    mu = lw['mu_shift']
    mu_sm = jnp.concatenate([_pad_cols(mu[None, o_wlr:o_alr], LANES), _pad_cols(mu[None, o_alr:o_glr], LANES),
                             mu[None, o_glr:]], axis=1)
    rowv = lambda v: v.reshape(1, -1)
    o_b = _rwkv(proj, mu[None, 0:d], mu[None, d:2 * d], mu[None, 2 * d:3 * d], mu_sm,
                _pad_rows(lw['rwkv_ww2'], LANES).astype(BF16), _pad_rows(lw['rwkv_wa2'], LANES).astype(BF16),
                lw['rwkv_wg2'].astype(BF16), rowv(lw['rwkv_w0']), rowv(lw['rwkv_a0']), rowv(lw['rwkv_kk']),
                rowv(lw['rwkv_ka']), rowv(lw['rwkv_rk']), rowv(lw['rwkv_lnw']), rowv(lw['rwkv_lnb']),
                batch=batch, seq=seq, d_model=d, tb=tiles['rwkv_tb'], col_r=col_r, col_small=col_small)

    x1, h2p, logits_t = _merge(x2d, o_a, o_b, proj, lw['w_o'].astype(BF16), lw['norm_ffn'][None, :],
                               jnp.transpose(lw['w_router']), tm=tiles['merge_tm'], col_mga=col_mga)

    eidx, gate8 = _route(logits_t, lw['router_bias'][:, None], tr=tiles['route_tr'])
    toks, dests, block_e, n_valid = _dispatch_tables(eidx, n)
    y8 = _experts(h2p, toks, dests, block_e, n_valid, lw['w_exp_gate'], lw['w_exp_up'], lw['w_exp_down'])
    x2 = _combine(x1, h2p, gate8, y8, lw['w_sh_gate'].astype(BF16), lw['w_sh_up'].astype(BF16),
                  lw['w_sh_down'].astype(BF16), tm=tiles['combine_tm'])
    return x2


_TILES = dict(in_tm=1024, in_tn=1024, gla_tb=512, rwkv_tb=512, merge_tm=256, route_tr=512,
              combine_tm=256, ple_tm=512)


def _forward(x, p, weights, norm_final, tiles):
    batch, seq, d = x.shape
    n = batch * seq
    depth = p.shape[0]
    x2d = x.reshape(n, d)
    for i in range(depth):
        lw = {k: v[i] for k, v in weights.items()}
        x2 = _layer(x2d, p[i].reshape(n, -1), lw, batch=batch, seq=seq, tiles=tiles)
        last = norm_final[None, :] if i == depth - 1 else None
        x2d = _ple(x2, p[i].reshape(n, -1), lw['norm_ple'][None, :], lw['w_ple_gate'].astype(BF16),
                   lw['w_ple_proj'].astype(BF16), last, tm=tiles['ple_tm'])
    return x2d.reshape(batch, seq, d)


def kernel(x, p, norm_mix, w_in, mu_shift, gla_wa2, gla_ba, gla_onorm, rwkv_w0, rwkv_ww2, rwkv_a0, rwkv_wa2, rwkv_wg2, rwkv_kk, rwkv_ka, rwkv_rk, rwkv_lnw, rwkv_lnb, w_o, norm_ffn, w_router, router_bias, w_exp_gate, w_exp_up, w_exp_down, w_sh_gate, w_sh_up, w_sh_down, norm_ple, w_ple_gate, w_ple_proj, norm_final):
    weights = dict(norm_mix=norm_mix, w_in=w_in, mu_shift=mu_shift, gla_wa2=gla_wa2, gla_ba=gla_ba,
                   gla_onorm=gla_onorm, rwkv_w0=rwkv_w0, rwkv_ww2=rwkv_ww2, rwkv_a0=rwkv_a0, rwkv_wa2=rwkv_wa2,
                   rwkv_wg2=rwkv_wg2, rwkv_kk=rwkv_kk, rwkv_ka=rwkv_ka, rwkv_rk=rwkv_rk, rwkv_lnw=rwkv_lnw,
                   rwkv_lnb=rwkv_lnb, w_o=w_o, norm_ffn=norm_ffn, w_router=w_router, router_bias=router_bias,
                   w_exp_gate=w_exp_gate, w_exp_up=w_exp_up, w_exp_down=w_exp_down, w_sh_gate=w_sh_gate,
                   w_sh_up=w_sh_up, w_sh_down=w_sh_down, norm_ple=norm_ple, w_ple_gate=w_ple_gate,
                   w_ple_proj=w_ple_proj)
    return _forward(x, p, weights, norm_final, _TILES)
```

```python
import functools

import jax
import jax.numpy as jnp
from jax import lax
from jax.experimental import pallas as pl
from jax.experimental.pallas import tpu as pltpu

F32 = jnp.float32
BF16 = jnp.bfloat16
HIGHEST = lax.Precision.HIGHEST

RMS_EPS = 1e-6
GLA_HEADS = 4
GLA_GATE_RANK = 16
GLA_TAU = 16.0
GLA_NORM_EPS = 1e-5
RWKV_HEAD = 64
RWKV_LN_EPS = 64e-5
N_EXPERTS = 64
TOP_K = 8
N_GROUPS = 8
TOPK_GROUPS = 4
ROUTED_SCALE = 2.5
MOE_BLOCK = 256
CHUNK = 64
LANES = 128
VMEM_LIMIT = 56 * 1024 * 1024

NT = (((1,), (1,)), ((), ()))
TN = (((0,), (0,)), ((), ()))


def _params(sem, vmem=VMEM_LIMIT):
    return pltpu.CompilerParams(dimension_semantics=sem, vmem_limit_bytes=vmem)


def _bdot(a, b):
    return jnp.dot(a.astype(BF16), b.astype(BF16), preferred_element_type=F32)


def _bf16_pieces(x, n):
    pieces = []
    for _ in range(n):
        p = x.astype(BF16)
        pieces.append(p)
        x = x - p.astype(F32)
    return pieces


def _cumsum_rows(tri3, x):
    return jnp.dot(tri3, jnp.concatenate(_bf16_pieces(x, 3), axis=0), preferred_element_type=F32)


def _tri3(c):
    tri = (lax.broadcasted_iota(jnp.int32, (c, c), 0) >= lax.broadcasted_iota(jnp.int32, (c, c), 1)).astype(BF16)
    return jnp.concatenate([tri, tri, tri], axis=1)


def _pack_halves(x):
    w = x.shape[1] // 2
    bits = lax.bitcast_convert_type(x.astype(BF16).astype(F32), jnp.uint32)
    return (bits[:, :w] >> 16) | bits[:, w:]


def _unpack_halves(u):
    lo = lax.bitcast_convert_type(u << 16, F32)
    hi = lax.bitcast_convert_type(u & jnp.uint32(0xFFFF0000), F32)
    return lo, hi


def _softplus(y):
    return jnp.maximum(y, 0.0) + jnp.log1p(jnp.exp(-jnp.abs(y)))


def _in_proj_kernel(x_ref, g_ref, w_ref, o_ref, h_ref):
    @pl.when(pl.program_id(1) == 0)
    def _():
        x = x_ref[...]
        ms = jnp.mean(x * x, axis=-1, keepdims=True)
        h_ref[...] = (x * lax.rsqrt(ms + RMS_EPS) * g_ref[...]).astype(BF16)

    o_ref[...] = jnp.dot(h_ref[...], w_ref[...], preferred_element_type=F32).astype(o_ref.dtype)


def _in_proj(x2d, gain, w_cat, tm, tn):
    n, d = x2d.shape
    m = w_cat.shape[1]
    return pl.pallas_call(
        _in_proj_kernel,
        grid=(n // tm, m // tn),
        in_specs=[
            pl.BlockSpec((tm, d), lambda i, j: (i, 0)),
            pl.BlockSpec((1, d), lambda i, j: (0, 0)),
            pl.BlockSpec((d, tn), lambda i, j: (0, j)),
        ],
        out_specs=pl.BlockSpec((tm, tn), lambda i, j: (i, j)),
        out_shape=jax.ShapeDtypeStruct((n, m), BF16),
        scratch_shapes=[pltpu.VMEM((tm, d), BF16)],
        compiler_params=_params(("parallel", "arbitrary")),
        name="in_proj",
    )(x2d, gain, w_cat)


def _gla_kernel(q_ref, k_ref, v_ref, og_ref, glr_ref, wa2_ref, ba_ref, on_ref, o_ref, st_ref, *, dk):
    @pl.when(pl.program_id(2) == 0)
    def _():
        st_ref[...] = jnp.zeros_like(st_ref)

    tb = q_ref.shape[0]
    row = lax.broadcasted_iota(jnp.int32, (CHUNK, CHUNK), 0)
    col = lax.broadcasted_iota(jnp.int32, (CHUNK, CHUNK), 1)
    causal = row >= col
    tri3 = _tri3(CHUNK)
    scale = dk ** -0.5
    rows = [pl.ds(c * CHUNK, CHUNK) for c in range(tb // CHUNK)]
    la = jnp.dot(glr_ref[...], wa2_ref[...], preferred_element_type=F32) + ba_ref[...]
    la = (jnp.minimum(la, 0.0) - jnp.log1p(jnp.exp(-jnp.abs(la)))) / GLA_TAU
    bcums = [_cumsum_rows(tri3, la[c * CHUNK:(c + 1) * CHUNK]) for c in range(len(rows))]
    q_t, k_t, k_d, decay = [], [], [], []
    for sl, bcum in zip(rows, bcums):
        k = k_ref[sl, :].astype(F32)
        btot = bcum[CHUNK - 1:CHUNK, :]
        q_t.append((q_ref[sl, :].astype(F32) * scale * jnp.exp(bcum)).astype(BF16))
        k_t.append((k * jnp.exp(-bcum)).astype(BF16))
        k_d.append((k * jnp.exp(btot - bcum)).astype(BF16))
        decay.append(jnp.exp(btot))
    atts = [lax.dot_general(qt, kt, NT, preferred_element_type=F32) for qt, kt in zip(q_t, k_t)]
    atts = [jnp.where(causal, att, 0.0).astype(BF16) for att in atts]
    upds = [lax.dot_general(v_ref[sl, :], kd, TN, preferred_element_type=F32) for sl, kd in zip(rows, k_d)]
    states = [st_ref[...]]
    for dec, upd in zip(decay, upds):
        states.append(states[-1] * dec + upd)
    st_ref[...] = states[-1]
    outs = [jnp.dot(att, v_ref[sl, :], preferred_element_type=F32)
            + lax.dot_general(qt, st.astype(BF16), NT, preferred_element_type=F32)
            for att, sl, qt, st in zip(atts, rows, q_t, states)]
    for sl, o in zip(rows, outs):
        ms = jnp.mean(o * o, axis=-1, keepdims=True)
        o = o * lax.rsqrt(ms + GLA_NORM_EPS) * on_ref[...]
        og = og_ref[sl, :].astype(F32)
        o_ref[sl, :] = (o * (og * jax.nn.sigmoid(og))).astype(o_ref.dtype)


def _gla(proj, wa2p, ba, onorm, *, batch, seq, d_model, tb, col_small_glr):
    heads = GLA_HEADS
    dk = d_model // 2 // heads
    dv = d_model // heads
    n = batch * seq
    nt = seq // tb
    rowmap = lambda b, h, t: b * nt + t
    kcol = (d_model // 2) // dk
    vcol = d_model // dv
    ogcol = 2 * d_model // dv
    return pl.pallas_call(
        functools.partial(_gla_kernel, dk=dk),
        grid=(batch, heads, nt),
        in_specs=[
            pl.BlockSpec((tb, dk), lambda b, h, t: (rowmap(b, h, t), h)),
            pl.BlockSpec((tb, dk), lambda b, h, t: (rowmap(b, h, t), kcol + h)),
            pl.BlockSpec((tb, dv), lambda b, h, t: (rowmap(b, h, t), vcol + h)),
            pl.BlockSpec((tb, dv), lambda b, h, t: (rowmap(b, h, t), ogcol + h)),
            pl.BlockSpec((tb, LANES), lambda b, h, t: (rowmap(b, h, t), col_small_glr // LANES)),
            pl.BlockSpec((LANES, dk), lambda b, h, t: (0, h)),
            pl.BlockSpec((1, dk), lambda b, h, t: (0, h)),
            pl.BlockSpec((1, dv), lambda b, h, t: (0, 0)),
        ],
        out_specs=pl.BlockSpec((tb, dv), lambda b, h, t: (rowmap(b, h, t), h)),
        out_shape=jax.ShapeDtypeStruct((n, d_model), BF16),
        scratch_shapes=[pltpu.VMEM((dv, dk), F32)],
        compiler_params=_params(("parallel", "parallel", "arbitrary")),
        name="gla",
    )(proj, proj, proj, proj, proj, wa2p, ba, onorm)


def _shift_mix(z, prev_ref, mu, first_row):
    zs = pltpu.roll(z, 1, 0)
    zs = jnp.where(first_row, prev_ref[0:1, :], zs)
    prev_ref[0:1, :] = z[z.shape[0] - 1:z.shape[0], :]
    return z + (zs - z) * mu


def _rwkv_kernel(r_ref, k_ref, v_ref, sm_ref, mur_ref, muk_ref, muv_ref, musm_ref,
                 ww2_ref, wa2_ref, wg2_ref, w0_ref, a0_ref, kk_ref, ka_ref, rk_ref, lnw_ref, lnb_ref,
                 o_ref, ht_ref, pr_ref, pk_ref, pv_ref, psm_ref, qe_s, y0_s, mc_s, sa_s, dec_s, bon_s, g_s):
    t = pl.program_id(2)
    last = pl.num_programs(2) - 1
    tb, gw = r_ref.shape
    hd = RWKV_HEAD
    nc = tb // CHUNK

    @pl.when(t == 0)
    def _():
        for ref in (ht_ref, pr_ref, pk_ref, pv_ref, psm_ref, qe_s, y0_s, mc_s, sa_s, dec_s, bon_s, g_s):
            ref[...] = jnp.zeros_like(ref)

    lane_r = lax.broadcasted_iota(jnp.int32, (gw, gw), 0) // hd
    lane_c = lax.broadcasted_iota(jnp.int32, (gw, gw), 1) // hd
    head_sum = (lane_r == lane_c).astype(BF16)
    head_sum2 = jnp.concatenate([head_sum, head_sum], axis=0)

    def hsum(x):
        return jnp.dot(jnp.concatenate(_bf16_pieces(x, 2), axis=1), head_sum2, preferred_element_type=F32)

    def previous_block():
        state = dict(ht=ht_ref[...], ys=[], c=0)

        def advance(n):
            for c in range(state['c'], min(state['c'] + n, nc)):
                hb = state['ht'].astype(BF16)
                y_st = lax.dot_general(qe_s[c], hb, NT, preferred_element_type=F32) + y0_s[c]
                state['ys'].append(y_st[0:CHUNK] + y_st[CHUNK:])
                state['ht'] = (state['ht'] * dec_s[c, 0:1, :]
                               + jnp.dot(hb, mc_s[c], preferred_element_type=F32) + sa_s[c])
                state['c'] = c + 1

        def finish():
            advance(nc)
            ht_ref[...] = state['ht']
            y = jnp.concatenate(state['ys'], axis=0)
            mean = hsum(y) * (1.0 / hd)
            dlt = y - mean
            var = hsum(dlt * dlt) * (1.0 / hd)
            y = dlt * lax.rsqrt(var + RWKV_LN_EPS) * lnw_ref[...] + lnb_ref[...]
            o_ref[...] = ((y + bon_s[...]) * g_s[...]).astype(o_ref.dtype)

        return advance, finish

    @pl.when(t == last)
    def _():
        previous_block()[1]()

    @pl.when(t < last)
    def _():
        _rwkv_block(r_ref, k_ref, v_ref, sm_ref, mur_ref, muk_ref, muv_ref, musm_ref, ww2_ref, wa2_ref, wg2_ref,
                    w0_ref, a0_ref, kk_ref, ka_ref, rk_ref, pr_ref, pk_ref, pv_ref, psm_ref,
                    qe_s, y0_s, mc_s, sa_s, dec_s, bon_s, g_s, hsum, previous_block)


def _rwkv_block(r_ref, k_ref, v_ref, sm_ref, mur_ref, muk_ref, muv_ref, musm_ref, ww2_ref, wa2_ref, wg2_ref,
                w0_ref, a0_ref, kk_ref, ka_ref, rk_ref, pr_ref, pk_ref, pv_ref, psm_ref,
                qe_s, y0_s, mc_s, sa_s, dec_s, bon_s, g_s, hsum, previous_block):
    tb, gw = r_ref.shape
    hd = RWKV_HEAD
    nc = tb // CHUNK
    advance, finish = previous_block()
    per_stage = -(-nc // 8)

    first_row = lax.broadcasted_iota(jnp.int32, (tb, 1), 0) == 0
    r = _shift_mix(r_ref[...].astype(F32), pr_ref, mur_ref[...], first_row)
    k = _shift_mix(k_ref[...].astype(F32), pk_ref, muk_ref[...], first_row)
    v = _shift_mix(v_ref[...].astype(F32), pv_ref, muv_ref[...], first_row)
    sm = _shift_mix(sm_ref[...].astype(F32), psm_ref, musm_ref[...], first_row)
    w_lr = sm[:, 0:LANES]
    a_lr = sm[:, LANES:2 * LANES]
    g_lr = sm[:, 2 * LANES:]

    w_log = -_softplus(-(w0_ref[...] + _bdot(jnp.tanh(w_lr), ww2_ref[...]))) - 0.5
    lw = -jnp.exp(w_log)
    icl = jax.nn.sigmoid(a0_ref[...] + _bdot(a_lr, wa2_ref[...]))
    g = _bdot(jax.nn.sigmoid(g_lr), wg2_ref[...])

    kk = k * kk_ref[...]
    kk = kk / jnp.maximum(jnp.sqrt(hsum(kk * kk)), 1e-12)
    k2 = k * (1.0 + (icl - 1.0) * ka_ref[...])
    a = -kk
    b = kk * icl
    bonus = hsum(r * k2 * rk_ref[...]) * v

    head0 = lax.broadcasted_iota(jnp.int32, (1, gw), 1) < hd
    rr = lax.broadcasted_iota(jnp.int32, (gw, gw), 0) % CHUNK
    cc = lax.broadcasted_iota(jnp.int32, (gw, gw), 1) % CHUNK
    strict = rr > cc
    incl = rr >= cc
    eye = (lax.broadcasted_iota(jnp.int32, (gw, gw), 0)
           == lax.broadcasted_iota(jnp.int32, (gw, gw), 1)).astype(F32)
    tri3 = _tri3(CHUNK)

    def split_heads(z):
        return jnp.concatenate([jnp.where(head0, z, 0.0), jnp.where(head0, 0.0, z)], axis=0)

    rows = [slice(c * CHUNK, (c + 1) * CHUNK) for c in range(nc)]
    cums = [_cumsum_rows(tri3, lw[sl]) for sl in rows]
    a_st, r_st, v_st, bh_st, kh_st, lhs, rhs, decay = [], [], [], [], [], [], [], []
    for sl, cum in zip(rows, cums):
        tot = cum[CHUNK - 1:CHUNK, :]
        e_neg = jnp.exp(-cum)
        e_rem = jnp.exp(tot - cum)
        a_st.append(split_heads(a[sl] * jnp.exp(cum - lw[sl])))
        r_st.append(split_heads(r[sl] * jnp.exp(cum)))
        v_st.append(split_heads(v[sl]))
        bh_st.append(split_heads(b[sl] * e_rem).astype(BF16))
        kh_st.append(split_heads(k2[sl] * e_rem).astype(BF16))
        lhs.append(jnp.concatenate([a_st[-1], r_st[-1]], axis=0).astype(BF16))
        rhs.append(jnp.concatenate([split_heads(b[sl] * e_neg), split_heads(k2[sl] * e_neg)],
                                   axis=0).astype(BF16))
        decay.append(jnp.exp(tot))
    pms = [lax.dot_general(lh, rh, NT, preferred_element_type=F32) for lh, rh in zip(lhs, rhs)]
    advance(per_stage)
    a_ab = [jnp.where(strict, pm[0:gw, 0:gw], 0.0) for pm in pms]
    a_ak = [jnp.where(strict, pm[0:gw, gw:], 0.0) for pm in pms]
    a_rb = [jnp.where(incl, pm[gw:, 0:gw], 0.0) for pm in pms]
    a_rk = [jnp.where(incl, pm[gw:, gw:], 0.0) for pm in pms]

    xs = a_ab
    t_inv = [eye + m for m in a_ab]
    for _ in range(5):
        xs = [_bdot(x, x) for x in xs]
        t_inv = [t + _bdot(t, x) for t, x in zip(t_inv, xs)]
        advance(per_stage)

    akv = [_bdot(m, vs) for m, vs in zip(a_ak, v_st)]
    advance(per_stage)
    zs = [_bdot(t, jnp.concatenate([as_, kv], axis=1)) for t, as_, kv in zip(t_inv, a_st, akv)]
    advance(per_stage)
    q_eff = [(rs + _bdot(m, z[:, 0:gw])).astype(BF16) for rs, m, z in zip(r_st, a_rb, zs)]
    y0 = [_bdot(jnp.concatenate([mb, mk], axis=1), jnp.concatenate([z[:, gw:], vs], axis=0))
          for mb, mk, z, vs in zip(a_rb, a_rk, zs, v_st)]
    v_t = [jnp.transpose(vs) for vs in v_st]
    z_t = [jnp.transpose(z) for z in zs]
    mzs = [_bdot(zt, bh) for zt, bh in zip(z_t, bh_st)]
    m_corr = [mz[0:gw].astype(BF16) for mz in mzs]
    s_add = [mz[gw:] + _bdot(vt, kh) for mz, vt, kh in zip(mzs, v_t, kh_st)]

    finish()
    for c in range(nc):
        qe_s[c] = q_eff[c]
        y0_s[c] = y0[c]
        mc_s[c] = m_corr[c]
        sa_s[c] = s_add[c]
        dec_s[c, 0:1, :] = decay[c]
    bon_s[...] = bonus
    g_s[...] = g


def _rwkv(proj, mu_r, mu_k, mu_v, mu_sm, ww2p, wa2p, wg2, w0, a0, kkp, kap, rkp, lnw, lnb,
          *, batch, seq, d_model, tb, col_r, col_small):
    n = batch * seq
    nt = seq // tb
    gw = LANES
    ng = d_model // gw
    smw = mu_sm.shape[1]
    nc = tb // CHUNK
    rowmap = lambda b, g, t: b * nt + jnp.minimum(t, nt - 1)
    seg = lambda off: (lambda b, g, t: (rowmap(b, g, t), off // gw + g))
    vec = pl.BlockSpec((1, gw), lambda b, g, t: (0, g))
    return pl.pallas_call(
        _rwkv_kernel,
        grid=(batch, ng, nt + 1),
        in_specs=[
            pl.BlockSpec((tb, gw), seg(col_r)),
            pl.BlockSpec((tb, gw), seg(col_r + d_model)),
            pl.BlockSpec((tb, gw), seg(col_r + 2 * d_model)),
            pl.BlockSpec((tb, smw), lambda b, g, t: (rowmap(b, g, t), col_small // smw)),
            vec, vec, vec,
            pl.BlockSpec((1, smw), lambda b, g, t: (0, 0)),
            pl.BlockSpec((LANES, gw), lambda b, g, t: (0, g)),
            pl.BlockSpec((LANES, gw), lambda b, g, t: (0, g)),
            pl.BlockSpec((wg2.shape[0], gw), lambda b, g, t: (0, g)),
            vec, vec, vec, vec, vec, vec, vec,
        ],
        out_specs=pl.BlockSpec((tb, gw), lambda b, g, t: (b * nt + jnp.maximum(t - 1, 0), g)),
        out_shape=jax.ShapeDtypeStruct((n, d_model), BF16),
        scratch_shapes=[
            pltpu.VMEM((gw, gw), F32),
            pltpu.VMEM((8, gw), F32), pltpu.VMEM((8, gw), F32), pltpu.VMEM((8, gw), F32),
            pltpu.VMEM((8, smw), F32),
            pltpu.VMEM((nc, gw, gw), BF16), pltpu.VMEM((nc, gw, gw), F32),
            pltpu.VMEM((nc, gw, gw), BF16), pltpu.VMEM((nc, gw, gw), F32),
            pltpu.VMEM((nc, 8, gw), F32),
            pltpu.VMEM((tb, gw), F32), pltpu.VMEM((tb, gw), F32),
        ],
        compiler_params=_params(("parallel", "parallel", "arbitrary")),
        name="rwkv",
    )(proj, proj, proj, proj, mu_r, mu_k, mu_v, mu_sm, ww2p, wa2p, wg2, w0, a0, kkp, kap, rkp, lnw, lnb)


MERGE_PARTS = 2


def _merge_kernel(x_ref, oa_ref, ob_ref, ga_ref, gb_ref, wo_ref, g_ref, wr_ref, x1_ref, h2_ref, lg_ref):
    tm = x_ref.shape[0]
    parts = [pl.ds(i * (tm // MERGE_PARTS), tm // MERGE_PARTS) for i in range(MERGE_PARTS)]
    mixed = [(jax.nn.sigmoid(ga_ref[p, :].astype(F32)) * oa_ref[p, :].astype(F32)
              + jax.nn.sigmoid(gb_ref[p, :].astype(F32)) * ob_ref[p, :].astype(F32)).astype(BF16) for p in parts]
    x1s = [x_ref[p, :] + jnp.dot(m, wo_ref[...], preferred_element_type=F32) for p, m in zip(parts, mixed)]
    h2s = []
    for p, x1 in zip(parts, x1s):
        x1_ref[p, :] = x1
        ms = jnp.mean(x1 * x1, axis=-1, keepdims=True)
        h2 = x1 * lax.rsqrt(ms + RMS_EPS) * g_ref[...]
        h2_ref[p, :] = _pack_halves(h2)
        h2s.append(h2)
    lg_ref[...] = jnp.concatenate(
        [lax.dot_general(wr_ref[...], h2, NT, preferred_element_type=F32, precision=HIGHEST) for h2 in h2s], axis=1)


def _merge(x2d, o_a, o_b, proj, w_o, gain, w_router_t, *, tm, col_mga):
    n, d = x2d.shape
    ne = w_router_t.shape[0]
    row = lambda i: (i, 0)
    return pl.pallas_call(
        _merge_kernel,
        grid=(n // tm,),
        in_specs=[
            pl.BlockSpec((tm, d), row),
            pl.BlockSpec((tm, d), row),
            pl.BlockSpec((tm, d), row),
            pl.BlockSpec((tm, d), lambda i: (i, col_mga // d)),
            pl.BlockSpec((tm, d), lambda i: (i, col_mga // d + 1)),
            pl.BlockSpec((d, d), lambda i: (0, 0)),
            pl.BlockSpec((1, d), lambda i: (0, 0)),
            pl.BlockSpec((ne, d), lambda i: (0, 0)),
        ],
        out_specs=[
            pl.BlockSpec((tm, d), row),
            pl.BlockSpec((tm, d // 2), row),
            pl.BlockSpec((ne, tm), lambda i: (0, i)),
        ],
        out_shape=[
            jax.ShapeDtypeStruct((n, d), F32),
            jax.ShapeDtypeStruct((n, d // 2), jnp.uint32),
            jax.ShapeDtypeStruct((ne, n), F32),
        ],
        compiler_params=_params(("parallel",)),
        name="merge",
    )(x2d, o_a, o_b, proj, proj, w_o, gain, w_router_t)


def _route_kernel(lg_ref, bias_ref, eidx_ref, gate_ref):
    ne, tr = lg_ref.shape
    gsz = ne // N_GROUPS
    neg = -jnp.inf
    scores = jax.nn.sigmoid(lg_ref[...])
    biased = scores + bias_ref[...]

    def first_max(vals, iota, size):
        m = jnp.max(vals, axis=0, keepdims=True)
        return m, jnp.min(jnp.where(vals == m, iota, float(size)), axis=0, keepdims=True)

    in_iota = lax.broadcasted_iota(jnp.int32, (gsz, tr), 0).astype(F32)
    slabs, gs_rows = [], []
    for grp in range(N_GROUPS):
        slab = biased[grp * gsz:(grp + 1) * gsz, :]
        m1, i1 = first_max(slab, in_iota, gsz)
        m2 = jnp.max(jnp.where(in_iota == i1, neg, slab), axis=0, keepdims=True)
        slabs.append(slab)
        gs_rows.append(m1 + m2)
    gs = jnp.concatenate(gs_rows, axis=0)
    g_iota = lax.broadcasted_iota(jnp.int32, (N_GROUPS, tr), 0).astype(F32)
    gsel = jnp.zeros((N_GROUPS, tr), F32)
    for _ in range(TOPK_GROUPS):
        _, gi = first_max(gs, g_iota, N_GROUPS)
        hit = g_iota == gi
        gsel = jnp.where(hit, 1.0, gsel)
        gs = jnp.where(hit, neg, gs)
    cand = jnp.concatenate(
        [jnp.where(gsel[grp:grp + 1, :] > 0.0, slabs[grp], neg) for grp in range(N_GROUPS)], axis=0)
    e_iota = lax.broadcasted_iota(jnp.int32, (ne, tr), 0).astype(F32)
    idxs, tops = [], []
    for _ in range(TOP_K):
        _, ei = first_max(cand, e_iota, ne)
        hit = e_iota == ei
        idxs.append(ei)
        tops.append(jnp.sum(jnp.where(hit, scores, 0.0), axis=0, keepdims=True))
        cand = jnp.where(hit, neg, cand)
    top_s = jnp.concatenate(tops, axis=0)
    eidx_ref[...] = jnp.concatenate(idxs, axis=0).astype(jnp.int32)
    gate_ref[...] = top_s / jnp.sum(top_s, axis=0, keepdims=True) * ROUTED_SCALE


def _route(logits_t, bias_col, *, tr):
    ne, n = logits_t.shape
    return pl.pallas_call(
        _route_kernel,
        grid=(n // tr,),
        in_specs=[pl.BlockSpec((ne, tr), lambda i: (0, i)), pl.BlockSpec((ne, 1), lambda i: (0, 0))],
        out_specs=[pl.BlockSpec((TOP_K, tr), lambda i: (0, i)), pl.BlockSpec((TOP_K, tr), lambda i: (0, i))],
        out_shape=[jax.ShapeDtypeStruct((TOP_K, n), jnp.int32), jax.ShapeDtypeStruct((TOP_K, n), F32)],
        compiler_params=_params(("parallel",)),
        name="route",
    )(logits_t, bias_col)


def _dispatch_tables(eidx, n_tokens):
    n = n_tokens
    nk = n * TOP_K
    n_blocks = -(-nk // MOE_BLOCK) + N_EXPERTS
    tok = jnp.broadcast_to(jnp.arange(n, dtype=jnp.int32)[None, :], (TOP_K, n))
    krow = jnp.arange(TOP_K, dtype=jnp.int32)[:, None]
    real_key = (eidx * (2 * n) + tok).reshape(nk)
    real_dest = (krow * n + tok).reshape(nk)
    counts = jnp.sum(eidx.reshape(1, nk) == jnp.arange(N_EXPERTS, dtype=jnp.int32)[:, None], axis=1)
    padded = (counts + MOE_BLOCK - 1) // MOE_BLOCK * MOE_BLOCK
    n_pad = n_blocks * MOE_BLOCK - nk
    pad_e = jnp.arange(n_pad, dtype=jnp.int32) // MOE_BLOCK
    pad_i = jnp.arange(n_pad, dtype=jnp.int32) % MOE_BLOCK
    pad_used = pad_i < (padded - counts)[jnp.minimum(pad_e, N_EXPERTS - 1)]
    pad_used = jnp.logical_and(pad_used, pad_e < N_EXPERTS)
    pad_key = jnp.where(pad_used, pad_e * (2 * n) + n + pad_i, jnp.iinfo(jnp.int32).max)
    pad_dest = nk + pad_i
    keys, dests = lax.sort_key_val(jnp.concatenate([real_key, pad_key]), jnp.concatenate([real_dest, pad_dest]))
    toks = jnp.minimum(keys % (2 * n), n - 1)
    pend_blocks = jnp.cumsum(padded) // MOE_BLOCK
    block_e = jnp.minimum(
        jnp.sum(pend_blocks[None, :] <= jnp.arange(n_blocks, dtype=jnp.int32)[:, None], axis=1), N_EXPERTS - 1)
    n_valid = pend_blocks[-1:]
    return (toks.reshape(n_blocks, 1, MOE_BLOCK), dests.reshape(n_blocks, 1, MOE_BLOCK),
            block_e.astype(jnp.int32), n_valid.astype(jnp.int32))


N_SLOTS = 3
N_PIECES = 8


def _pack_pair(lo, hi):
    bits = lambda z: lax.bitcast_convert_type(z.astype(BF16).astype(F32), jnp.uint32)
    return (bits(lo) >> 16) | bits(hi)


def _experts_kernel(be_ref, nv_ref, tok0_ref, tok1_ref, tok2_ref, dest_ref, destp_ref, h_hbm, wg_ref, wu_ref,
                    wd_ref, out_hbm, xbuf, ybuf, wgb, wub, wdb, hg_acc, hu_acc, hb_scr, gsem, ssem, *, n_tokens):
    j = pl.program_id(0)
    nv = nv_ref[0]
    slot = j % N_SLOTS
    rows_per_piece = -(-MOE_BLOCK // N_PIECES)

    def gather_rows(toks, s, rows):
        for r in rows:
            pltpu.make_async_copy(h_hbm.at[pl.ds(toks[0, r], 1), :], xbuf.at[s, pl.ds(r, 1), :],
                                  gsem.at[s]).start(priority=r % 2)

    def scatter_rows(dests, s, rows):
        for r in rows:
            pltpu.make_async_copy(ybuf.at[s, pl.ds(r, 1), :], out_hbm.at[pl.ds(dests[0, r], 1), :],
                                  ssem.at[s]).start(priority=r % 2)

    def gather_wait(s):
        pltpu.make_async_copy(h_hbm.at[pl.ds(0, MOE_BLOCK), :], xbuf.at[s], gsem.at[s]).wait()

    def scatter_wait(s):
        pltpu.make_async_copy(ybuf.at[s], out_hbm.at[pl.ds(0, MOE_BLOCK), :], ssem.at[s]).wait()

    def piece(k, s):
        rows = range(k * rows_per_piece, min(MOE_BLOCK, (k + 1) * rows_per_piece))
        nxt = (s + 2) % N_SLOTS
        gather_rows(tok2_ref, nxt, rows)
        pl.when(j >= 1)(lambda: scatter_rows(destp_ref, nxt, rows))
        half = wgb.shape[0] // 2
        n_in = N_PIECES // 2
        if k < n_in:
            w = xbuf.shape[2] // n_in
            cols = slice(k * w, (k + 1) * w)
            lo, hi = (z.astype(BF16) for z in _unpack_halves(xbuf[s, :, cols]))
            lo_rows, hi_rows = slice(k * w, (k + 1) * w), slice(half + k * w, half + (k + 1) * w)
            g = (jnp.dot(lo, wgb[lo_rows, :], preferred_element_type=F32)
                 + jnp.dot(hi, wgb[hi_rows, :], preferred_element_type=F32))
            u = (jnp.dot(lo, wub[lo_rows, :], preferred_element_type=F32)
                 + jnp.dot(hi, wub[hi_rows, :], preferred_element_type=F32))
            if k == 0:
                hg_acc[...] = g
                hu_acc[...] = u
            else:
                hg_acc[...] += g
                hu_acc[...] += u
        else:
            c = k - n_in
            if c == 0:
                hg = hg_acc[...]
                hb_scr[...] = (hg * jax.nn.sigmoid(hg) * hu_acc[...]).astype(BF16)
            w = ybuf.shape[2] // (N_PIECES - n_in)
            half_out = wdb.shape[1] // 2
            hb = hb_scr[...]
            y_lo = jnp.dot(hb, wdb[:, c * w:(c + 1) * w], preferred_element_type=F32)
            y_hi = jnp.dot(hb, wdb[:, half_out + c * w:half_out + (c + 1) * w], preferred_element_type=F32)
            ybuf[s, :, c * w:(c + 1) * w] = _pack_pair(y_lo, y_hi)

    def block_step(s):
        gather_wait(s)
        for k in range(N_PIECES):
            pl.when(j >= 0)(functools.partial(piece, k, s))

    @pl.when(j == 0)
    def _():
        ybuf[1] = jnp.zeros(ybuf.shape[1:], ybuf.dtype)
        spare = pltpu.make_async_copy(ybuf.at[1], out_hbm.at[pl.ds(TOP_K * n_tokens, MOE_BLOCK), :], ssem.at[1])
        spare.start()
        spare.wait()

    @pl.when(j < nv)
    def _():
        @pl.when(j == 0)
        def _():
            gather_rows(tok0_ref, 0, range(MOE_BLOCK))
            gather_rows(tok1_ref, 1, range(MOE_BLOCK))

        @pl.when(jnp.logical_or(j == 0, be_ref[j] != be_ref[jnp.maximum(j - 1, 0)]))
        def _():
            wgb[...] = wg_ref[...].astype(BF16)
            wub[...] = wu_ref[...].astype(BF16)
            wdb[...] = wd_ref[...].astype(BF16)

        @pl.when(j >= N_SLOTS)
        def _():
            scatter_wait(slot)

        for s in range(N_SLOTS):
            pl.when(slot == s)(functools.partial(block_step, s))

        @pl.when(j == nv - 1)
        def _():
            for s in range(N_SLOTS):
                pl.when(slot == s)(functools.partial(scatter_rows, dest_ref, s, range(MOE_BLOCK)))
            gather_wait((j + 1) % N_SLOTS)
            gather_wait((j + 2) % N_SLOTS)
            scatter_wait(slot)
            for back in range(1, N_SLOTS):
                pl.when(j >= back)(functools.partial(scatter_wait, (j + N_SLOTS - back) % N_SLOTS))


def _experts(h2p, toks, dests, block_e, n_valid, wg, wu, wd):
    n, dp = h2p.shape
    d = 2 * dp
    n_blocks = toks.shape[0]
    ff = wg.shape[2]

    def smem_blk(ahead):
        return pl.BlockSpec((None, 1, MOE_BLOCK),
                            lambda j, be, nv: (jnp.clip(j + ahead, 0, n_blocks - 1), 0, 0), memory_space=pltpu.SMEM)

    grid_spec = pltpu.PrefetchScalarGridSpec(
        num_scalar_prefetch=2,
        grid=(n_blocks,),
        in_specs=[
            smem_blk(0), smem_blk(1), smem_blk(2), smem_blk(0), smem_blk(-1),
            pl.BlockSpec(memory_space=pl.ANY),
            pl.BlockSpec((None, d, ff), lambda j, be, nv: (be[j], 0, 0)),
            pl.BlockSpec((None, d, ff), lambda j, be, nv: (be[j], 0, 0)),
            pl.BlockSpec((None, ff, d), lambda j, be, nv: (be[j], 0, 0)),
        ],
        out_specs=pl.BlockSpec(memory_space=pl.ANY),
        scratch_shapes=[
            pltpu.VMEM((N_SLOTS, MOE_BLOCK, dp), jnp.uint32),
            pltpu.VMEM((N_SLOTS, MOE_BLOCK, dp), jnp.uint32),
            pltpu.VMEM((d, ff), BF16),
            pltpu.VMEM((d, ff), BF16),
            pltpu.VMEM((ff, d), BF16),
            pltpu.VMEM((MOE_BLOCK, ff), F32),
            pltpu.VMEM((MOE_BLOCK, ff), F32),
            pltpu.VMEM((MOE_BLOCK, ff), BF16),
            pltpu.SemaphoreType.DMA((N_SLOTS,)),
            pltpu.SemaphoreType.DMA((N_SLOTS,)),
        ],
    )
    return pl.pallas_call(
        functools.partial(_experts_kernel, n_tokens=n),
        grid_spec=grid_spec,
        out_shape=jax.ShapeDtypeStruct((TOP_K * n + MOE_BLOCK, dp), jnp.uint32),
        compiler_params=_params(("arbitrary",)),
        name="experts",
    )(block_e, n_valid, toks, toks, toks, dests, dests, h2p, wg, wu, wd)


def _combine_kernel(x1_ref, h2_ref, gate_ref, *rest):
    y_refs = rest[:TOP_K]
    wg_ref, wu_ref, wd_ref, p_ref, gp_ref, wpg_ref, wpp_ref = rest[TOP_K:TOP_K + 7]
    gf_ref = rest[TOP_K + 7] if len(rest) == TOP_K + 9 else None
    o_ref = rest[-1]
    gates = jnp.transpose(gate_ref[...])
    half = x1_ref.shape[1] // 2
    acc_lo = x1_ref[:, 0:half]
    acc_hi = x1_ref[:, half:]
    for kk in range(TOP_K):
        y_lo, y_hi = _unpack_halves(y_refs[kk][...])
        acc_lo = acc_lo + gates[:, kk:kk + 1] * y_lo
        acc_hi = acc_hi + gates[:, kk:kk + 1] * y_hi
    h_lo, h_hi = (z.astype(BF16) for z in _unpack_halves(h2_ref[...]))
    hg = (jnp.dot(h_lo, wg_ref[0:half, :], preferred_element_type=F32)
          + jnp.dot(h_hi, wg_ref[half:, :], preferred_element_type=F32))
    hu = (jnp.dot(h_lo, wu_ref[0:half, :], preferred_element_type=F32)
          + jnp.dot(h_hi, wu_ref[half:, :], preferred_element_type=F32))
    sh = jnp.dot((hg * jax.nn.sigmoid(hg) * hu).astype(BF16), wd_ref[...], preferred_element_type=F32)
    x = jnp.concatenate([acc_lo, acc_hi], axis=1) + sh

    ms = jnp.mean(x * x, axis=-1, keepdims=True)
    hn = (x * lax.rsqrt(ms + RMS_EPS) * gp_ref[...]).astype(BF16)
    gate = jax.nn.sigmoid(jnp.dot(hn, wpg_ref[...], preferred_element_type=F32))
    emb = jnp.dot(p_ref[...].astype(BF16), wpp_ref[...], preferred_element_type=F32)
    x = x + gate * emb
    if gf_ref is not None:
        ms = jnp.mean(x * x, axis=-1, keepdims=True)
        x = x * lax.rsqrt(ms + RMS_EPS) * gf_ref[...]
    o_ref[...] = x


def _combine(x1, h2p, gate8, y8, wsg, wsu, wsd, p2d, g_ple, w_pg, w_pp, g_final, *, tm):
    n, d = x1.shape
    ff = wsg.shape[1]
    pd = p2d.shape[1]
    nb = n // tm
    row = lambda i: (i, 0)
    once = lambda shape: pl.BlockSpec(shape, lambda i: (0, 0), pipeline_mode=pl.Buffered(1))
    y_specs = [pl.BlockSpec((tm, d // 2), functools.partial(lambda i, kk: (kk * nb + i, 0), kk=kk))
               for kk in range(TOP_K)]
    final = [] if g_final is None else [g_final]
    return pl.pallas_call(
        _combine_kernel,
        grid=(nb,),
        in_specs=[pl.BlockSpec((tm, d), row), pl.BlockSpec((tm, d // 2), row),
                  pl.BlockSpec((TOP_K, tm), lambda i: (0, i))] + y_specs + [
                  once((d, ff)), once((d, ff)), once((ff, d)),
                  pl.BlockSpec((tm, pd), row), once((1, d)), once((d, d)), once((pd, d))]
                 + [once((1, d))] * len(final),
        out_specs=pl.BlockSpec((tm, d), row),
        out_shape=jax.ShapeDtypeStruct((n, d), F32),
        compiler_params=_params(("parallel",)),
        name="combine",
    )(x1, h2p, gate8, *([y8] * TOP_K), wsg, wsu, wsd, p2d, g_ple, w_pg, w_pp, *final)


def _pad_cols(w, width):
    return jnp.pad(w, ((0, 0), (0, width - w.shape[1])))


def _pad_rows(w, height):
    return jnp.pad(w, ((0, height - w.shape[0]), (0, 0)))


def _layer(x2d, p2d, lw, g_final, *, batch, seq, tiles):
    n, d = x2d.shape
    dk_all = d // 2
    w_in = lw['w_in']
    decay_rank = lw['rwkv_ww2'].shape[0]
    a_rank = lw['rwkv_wa2'].shape[0]
    g_rank = lw['rwkv_wg2'].shape[0]
    rw_width = 3 * d + decay_rank + a_rank + g_rank
    sizes = (dk_all, dk_all, d, GLA_GATE_RANK, d, rw_width, d, d)
    offs = [0]
    for s in sizes:
        offs.append(offs[-1] + s)
    seg = lambda i: w_in[:, offs[i]:offs[i + 1]]
    w_q, w_k, w_v, w_glr, w_og, w_rw, w_mga, w_mgb = (seg(i) for i in range(8))
    o_wlr, o_alr, o_glr = 3 * d, 3 * d + decay_rank, 3 * d + decay_rank + a_rank
    small = jnp.concatenate([
        _pad_cols(w_rw[:, o_wlr:o_alr], LANES), _pad_cols(w_rw[:, o_alr:o_glr], LANES), w_rw[:, o_glr:],
        _pad_cols(w_glr, LANES)], axis=1)
    small_w = 2 * LANES + g_rank
    tn = tiles['in_tn']
    small = _pad_cols(small, -(-small.shape[1] // tn) * tn)
    col_mga = 3 * d
    col_r = 5 * d
    col_small = 8 * d
    w_cat = jnp.concatenate([w_q, w_k, w_v, w_og, w_mga, w_mgb, w_rw[:, :3 * d], small], axis=1).astype(BF16)

    proj = _in_proj(x2d, lw['norm_mix'][None, :], w_cat, tiles['in_tm'], tn)

    o_a = _gla(proj, _pad_rows(lw['gla_wa2'], LANES).astype(BF16), lw['gla_ba'][None, :],
               lw['gla_onorm'][None, :], batch=batch, seq=seq, d_model=d, tb=tiles['gla_tb'],
               col_small_glr=col_small + small_w)

    mu = lw['mu_shift']
    mu_sm = jnp.concatenate([_pad_cols(mu[None, o_wlr:o_alr], LANES), _pad_cols(mu[None, o_alr:o_glr], LANES),
                             mu[None, o_glr:]], axis=1)
    rowv = lambda v: v.reshape(1, -1)
    o_b = _rwkv(proj, mu[None, 0:d], mu[None, d:2 * d], mu[None, 2 * d:3 * d], mu_sm,
                _pad_rows(lw['rwkv_ww2'], LANES).astype(BF16), _pad_rows(lw['rwkv_wa2'], LANES).astype(BF16),
                lw['rwkv_wg2'].astype(BF16), rowv(lw['rwkv_w0']), rowv(lw['rwkv_a0']), rowv(lw['rwkv_kk']),
                rowv(lw['rwkv_ka']), rowv(lw['rwkv_rk']), rowv(lw['rwkv_lnw']), rowv(lw['rwkv_lnb']),
                batch=batch, seq=seq, d_model=d, tb=tiles['rwkv_tb'], col_r=col_r, col_small=col_small)

    x1, h2p, logits_t = _merge(x2d, o_a, o_b, proj, lw['w_o'].astype(BF16), lw['norm_ffn'][None, :],
                               jnp.transpose(lw['w_router']), tm=tiles['merge_tm'], col_mga=col_mga)

    eidx, gate8 = _route(logits_t, lw['router_bias'][:, None], tr=tiles['route_tr'])
    toks, dests, block_e, n_valid = _dispatch_tables(eidx, n)
    y8 = _experts(h2p, toks, dests, block_e, n_valid, lw['w_exp_gate'], lw['w_exp_up'], lw['w_exp_down'])
    return _combine(x1, h2p, gate8, y8, lw['w_sh_gate'].astype(BF16), lw['w_sh_up'].astype(BF16),
                    lw['w_sh_down'].astype(BF16), p2d, lw['norm_ple'][None, :], lw['w_ple_gate'].astype(BF16),
                    lw['w_ple_proj'].astype(BF16), g_final, tm=tiles['combine_tm'])


_TILES = dict(in_tm=1024, in_tn=1024, gla_tb=512, rwkv_tb=512, merge_tm=256, route_tr=512,
              combine_tm=256)


def _forward(x, p, weights, norm_final, tiles):
    batch, seq, d = x.shape
    n = batch * seq
    depth = p.shape[0]
    x2d = x.reshape(n, d)
    for i in range(depth):
        lw = {k: (v.reshape(v.shape[1:]) if depth == 1 else v[i]) for k, v in weights.items()}
        last = norm_final[None, :] if i == depth - 1 else None
        x2d = _layer(x2d, p[i].reshape(n, -1), lw, last, batch=batch, seq=seq, tiles=tiles)
    return x2d.reshape(batch, seq, d)


def kernel(x, p, norm_mix, w_in, mu_shift, gla_wa2, gla_ba, gla_onorm, rwkv_w0, rwkv_ww2, rwkv_a0, rwkv_wa2, rwkv_wg2, rwkv_kk, rwkv_ka, rwkv_rk, rwkv_lnw, rwkv_lnb, w_o, norm_ffn, w_router, router_bias, w_exp_gate, w_exp_up, w_exp_down, w_sh_gate, w_sh_up, w_sh_down, norm_ple, w_ple_gate, w_ple_proj, norm_final):
    weights = dict(norm_mix=norm_mix, w_in=w_in, mu_shift=mu_shift, gla_wa2=gla_wa2, gla_ba=gla_ba,
                   gla_onorm=gla_onorm, rwkv_w0=rwkv_w0, rwkv_ww2=rwkv_ww2, rwkv_a0=rwkv_a0, rwkv_wa2=rwkv_wa2,
                   rwkv_wg2=rwkv_wg2, rwkv_kk=rwkv_kk, rwkv_ka=rwkv_ka, rwkv_rk=rwkv_rk, rwkv_lnw=rwkv_lnw,
                   rwkv_lnb=rwkv_lnb, w_o=w_o, norm_ffn=norm_ffn, w_router=w_router, router_bias=router_bias,
                   w_exp_gate=w_exp_gate, w_exp_up=w_exp_up, w_exp_down=w_exp_down, w_sh_gate=w_sh_gate,
                   w_sh_up=w_sh_up, w_sh_down=w_sh_down, norm_ple=norm_ple, w_ple_gate=w_ple_gate,
                   w_ple_proj=w_ple_proj)
    return _forward(x, p, weights, norm_final, _TILES)
```
